```python
import math, functools
import jax, jax.numpy as jnp
from jax import lax
import numpy as np

D_MODEL = 2048
BATCH = 2
SEQ = 4096
DEPTH = 1
DEC_BATCH = 128
DEC_SEQ = 4
PAST_LEN = 16384
PAGE_SIZE = 128

GDN_HEADS = 8
GDN_DK = 128
GDN_DV = 128
GDN_QK_W = GDN_HEADS * GDN_DK
GDN_V_W = GDN_HEADS * GDN_DV
GDN_CONV = 4
GDN_CHUNK = 64
CONV_CH = 2 * GDN_QK_W + GDN_V_W
MLA_HEADS = 8
Q_LORA = 512
KV_LORA = 512
QK_NOPE = 128
QK_ROPE = 64
V_HEAD = 128
MLA_V_W = MLA_HEADS * V_HEAD
MLA_SCALE = (QK_NOPE + QK_ROPE) ** -0.5
ROPE_BASE = 10000.0
ATTN_BLOCK = 128
D_FF = 5632
FFN_CONV = 3
PLE_DIM = 256
EPS = 1e-6
IN_SIZES = (CONV_CH, GDN_V_W, GDN_HEADS, GDN_HEADS, Q_LORA, KV_LORA, QK_ROPE, D_MODEL, D_MODEL)
N_IN = sum(IN_SIZES)

kernel_name = 'hybrid_gdn_mla_convffn_step'


def _split(x, sizes):
    return jnp.split(x, np.cumsum(sizes)[:-1].tolist(), axis=-1)


def rmsnorm(x, g):
    xf = x.astype(jnp.float32)
    y = xf * lax.rsqrt(jnp.mean(xf * xf, axis=-1, keepdims=True) + EPS)
    return (y * g.astype(jnp.float32)).astype(x.dtype)


def l2norm(x):
    xf = x.astype(jnp.float32)
    return xf * lax.rsqrt(jnp.sum(xf * xf, axis=-1, keepdims=True) + EPS)


def causal_dwconv(x, buf, w):
    k = w.shape[0]
    L = x.shape[1]
    xp = jnp.concatenate([buf.astype(x.dtype), x], axis=1)
    y = xp[:, 0:L] * w[0]
    for j in range(1, k):
        y = y + xp[:, j:j + L] * w[j]
    return y, xp[:, L:]


def rope_tables(pos):
    half = QK_ROPE // 2
    inv = jnp.exp(-math.log(ROPE_BASE) * jnp.arange(half, dtype=jnp.float32) / half)
    ang = pos.astype(jnp.float32)[:, None] * inv[None, :]
    return jnp.cos(ang), jnp.sin(ang)


def apply_rope(x, cos, sin):
    shape = (1, cos.shape[0]) + (1,) * (x.ndim - 3) + (cos.shape[1],)
    c = cos.reshape(shape)
    s = sin.reshape(shape)
    x1, x2 = jnp.split(x.astype(jnp.float32), 2, axis=-1)
    return jnp.concatenate([x1 * c - x2 * s, x2 * c + x1 * s], axis=-1).astype(x.dtype)


def gated_delta_rule(q, k, v, g, beta, S0):
    f32 = jnp.float32
    B, L, H, DK = q.shape
    DV = v.shape[-1]
    C = min(GDN_CHUNK, L)
    n = -(-L // C)
    pad = n * C - L

    def prep(t):
        t = t.astype(f32)
        t = jnp.pad(t, [(0, 0), (0, pad)] + [(0, 0)] * (t.ndim - 2))
        return jnp.swapaxes(t.reshape((B, n, C) + t.shape[2:]), 2, 3)

    q = prep(q) * (DK ** -0.5)
    k = prep(k)
    v = prep(v)
    g = prep(g)
    beta = prep(beta)
    gc = jnp.cumsum(g, axis=-1)
    tril = jnp.tril(jnp.ones((C, C), bool))
    strict = jnp.tril(jnp.ones((C, C), bool), -1)
    decay = jnp.exp(jnp.where(tril, gc[..., :, None] - gc[..., None, :], -jnp.inf))
    kb = k * beta[..., None]
    lmat = jnp.where(strict, jnp.einsum('bnhid,bnhjd->bnhij', kb, k) * decay, 0.0)
    amat = lmat + jnp.eye(C, dtype=f32)
    rhs = jnp.concatenate([v * beta[..., None], kb * jnp.exp(gc)[..., None]], axis=-1)
    sol = lax.linalg.triangular_solve(amat, rhs, left_side=True, lower=True, unit_diagonal=True)
    u = sol[..., :DV]
    w = sol[..., DV:]
    qk = jnp.einsum('bnhid,bnhjd->bnhij', q, k) * decay

    def step(S, xs):
        q_c, k_c, u_c, w_c, qk_c, gc_c = xs
        v_new = u_c - jnp.einsum('bhcd,bhde->bhce', w_c, S)
        o = (jnp.einsum('bhcd,bhde->bhce', q_c * jnp.exp(gc_c)[..., None], S)
             + jnp.einsum('bhij,bhje->bhie', qk_c, v_new))
        g_last = gc_c[..., -1]
        S = (S * jnp.exp(g_last)[..., None, None]
             + jnp.einsum('bhcd,bhce->bhde', k_c * jnp.exp(g_last[..., None] - gc_c)[..., None], v_new))
        return S, o

    xs = tuple(jnp.moveaxis(t, 1, 0) for t in (q, k, u, w, qk, gc))
    S, o = lax.scan(step, S0.astype(f32), xs)
    o = jnp.swapaxes(jnp.moveaxis(o, 0, 1), 2, 3).reshape(B, n * C, H, DV)[:, :L]
    return o, S


def gdn_branch(conv_in, z, b, a, conv_buf, S0, lw):
    B, L, _ = conv_in.shape
    y, new_buf = causal_dwconv(conv_in, conv_buf, lw['gdn_conv_w'])
    y = jax.nn.silu(y)
    q, k, v = _split(y, (GDN_QK_W, GDN_QK_W, GDN_V_W))
    q = l2norm(q.reshape(B, L, GDN_HEADS, GDN_DK))
    k = l2norm(k.reshape(B, L, GDN_HEADS, GDN_DK))
    v = v.reshape(B, L, GDN_HEADS, GDN_DV)
    beta = jax.nn.sigmoid(b.astype(jnp.float32))
    g = -jnp.exp(lw['gdn_A_log'].astype(jnp.float32)) * jax.nn.softplus(
        a.astype(jnp.float32) + lw['gdn_dt_bias'].astype(jnp.float32))
    o, S = gated_delta_rule(q, k, v, g, beta, S0)
    o = rmsnorm(o, lw['gdn_norm_w']) * jax.nn.silu(z.reshape(B, L, GDN_HEADS, GDN_DV).astype(jnp.float32))
    return o.reshape(B, L, GDN_V_W).astype(conv_in.dtype), S, new_buf


def mla_project(cq_raw, ckv_raw, kr_raw, cos, sin, lw):
    B, L, _ = cq_raw.shape
    cq = rmsnorm(cq_raw, lw['g_q_a'])
    q = (cq @ lw['w_uq']).reshape(B, L, MLA_HEADS, QK_NOPE + QK_ROPE)
    q_nope, q_rope = q[..., :QK_NOPE], q[..., QK_NOPE:]
    q_rope = apply_rope(q_rope, cos, sin)
    q_lat = jnp.einsum('blhd,chd->blhc', q_nope, lw['w_uk'])
    ckv = rmsnorm(ckv_raw, lw['g_kv_a'])
    kr = apply_rope(kr_raw, cos, sin)
    return q_lat, q_rope, ckv, kr


def mla_attend_prompt(q_lat, q_rope, ckv, kr):
    B, L, H, _ = q_lat.shape
    nb = L // ATTN_BLOCK
    qlb = jnp.moveaxis(q_lat.reshape(B, nb, ATTN_BLOCK, H, KV_LORA), 1, 0)
    qrb = jnp.moveaxis(q_rope.reshape(B, nb, ATTN_BLOCK, H, QK_ROPE), 1, 0)
    kpos = jnp.arange(L)

    def block(args):
        i, ql, qr = args
        s = (jnp.einsum('bqhc,bkc->bhqk', ql, ckv)
             + jnp.einsum('bqhr,bkr->bhqk', qr, kr)).astype(jnp.float32) * MLA_SCALE
        qpos = i * ATTN_BLOCK + jnp.arange(ATTN_BLOCK)
        s = jnp.where(kpos[None, :] <= qpos[:, None], s, -jnp.inf)
        p = jax.nn.softmax(s, axis=-1).astype(ckv.dtype)
        return jnp.einsum('bhqk,bkc->bqhc', p, ckv)

    o = lax.map(block, (jnp.arange(nb), qlb, qrb))
    return jnp.moveaxis(o, 0, 1).reshape(B, L, H, KV_LORA)


def mla_attend_sample(q_lat, q_rope, ckv_new, kr_new, cache_ckv, cache_kr, page_table, layer):
    T = q_lat.shape[1]
    causal = jnp.tril(jnp.ones((T, T), bool))

    def one(args):
        pages, ql, qr, cn, rn = args
        c_past = cache_ckv[layer, pages].reshape(-1, KV_LORA)
        r_past = cache_kr[layer, pages].reshape(-1, QK_ROPE)
        s_past = (jnp.einsum('thc,kc->htk', ql, c_past)
                  + jnp.einsum('thr,kr->htk', qr, r_past)).astype(jnp.float32) * MLA_SCALE
        s_new = (jnp.einsum('thc,kc->htk', ql, cn)
                 + jnp.einsum('thr,kr->htk', qr, rn)).astype(jnp.float32) * MLA_SCALE
        s_new = jnp.where(causal, s_new, -jnp.inf)
        p = jax.nn.softmax(jnp.concatenate([s_past, s_new], axis=-1), axis=-1)
        n_past = c_past.shape[0]
        return (jnp.einsum('htk,kc->thc', p[..., :n_past].astype(cn.dtype), c_past)
                + jnp.einsum('htk,kc->thc', p[..., n_past:].astype(cn.dtype), cn))

    return lax.map(one, (page_table, q_lat, q_rope, ckv_new, kr_new))


def trunk_layer(x, p_l, cos, sin, attend, gdn_conv_buf, gdn_S0, ffn_buf, lw):
    B, L, _ = x.shape
    h = rmsnorm(x, lw['g_mix'])
    conv_in, z, b, a, cq_raw, ckv_raw, kr_raw, gate_a, gate_b = _split(h @ lw['w_in'], IN_SIZES)
    o_gdn, S, new_gdn_buf = gdn_branch(conv_in, z, b, a, gdn_conv_buf, gdn_S0, lw)
    q_lat, q_rope, ckv, kr = mla_project(cq_raw, ckv_raw, kr_raw, cos, sin, lw)
    o_lat = attend(q_lat, q_rope, ckv, kr)
    o_mla = jnp.einsum('blhc,chd->blhd', o_lat, lw['w_uv']).reshape(B, L, MLA_V_W)
    merged = (jax.nn.sigmoid(gate_a) * (o_gdn @ lw['w_o_gdn'])
              + jax.nn.sigmoid(gate_b) * (o_mla @ lw['w_o_mla']))
    x = x + merged @ lw['w_out']
    h = rmsnorm(x, lw['g_ffn'])
    up, new_ffn_buf = causal_dwconv(h @ lw['w_up'], ffn_buf, lw['ffn_conv_w'])
    gt, val = jnp.split(up, 2, axis=-1)
    x = x + (jax.nn.silu(gt) * val) @ lw['w_down']
    gate = jax.nn.sigmoid(rmsnorm(x, lw['g_ple']) @ lw['w_ple_gate'])
    x = x + gate * (p_l @ lw['w_ple_proj'])
    return x, (ckv, kr, S, new_gdn_buf, new_ffn_buf)


def setup_inputs(seed: int = 0) -> dict:
    key = jax.random.key(seed)
    ks = iter(jax.random.split(key, 48))
    f32 = jnp.float32
    n_pages = PAST_LEN // PAGE_SIZE
    n_pool = (DEC_BATCH * n_pages * 5) // 4

    def nrm(shape, scale=1.0):
        return jax.random.normal(next(ks), shape, f32) * scale

    def gain(shape):
        return 1.0 + 0.05 * jax.random.normal(next(ks), shape, f32)

    perm = jax.random.permutation(next(ks), n_pool)[: DEC_BATCH * n_pages]
    page_table = perm.reshape(DEC_BATCH, n_pages).astype(jnp.int32)
    dt = jnp.exp(jax.random.uniform(next(ks), (DEPTH, GDN_HEADS), f32, math.log(1e-3), math.log(1e-1)))
    dt_bias = dt + jnp.log(-jnp.expm1(-dt))
    a_log = jnp.log(jax.random.uniform(next(ks), (DEPTH, GDN_HEADS), f32, 1.0, 16.0))
    return {
        'x_prompt': nrm((BATCH, SEQ, D_MODEL)),
        'x_sample': nrm((DEC_BATCH, DEC_SEQ, D_MODEL)),
        'cache_ckv': nrm((DEPTH, n_pool, PAGE_SIZE, KV_LORA)),
        'cache_krope': nrm((DEPTH, n_pool, PAGE_SIZE, QK_ROPE)),
        'page_table': page_table,
        'state_gdn': nrm((DEPTH, DEC_BATCH, GDN_HEADS, GDN_DK, GDN_DV)),
        'state_gdn_conv': nrm((DEPTH, DEC_BATCH, GDN_CONV - 1, CONV_CH)),
        'state_ffn_conv': nrm((DEPTH, DEC_BATCH, FFN_CONV - 1, 2 * D_FF)),
        'p_prompt': nrm((DEPTH, BATCH, SEQ, PLE_DIM)),
        'p_sample': nrm((DEPTH, DEC_BATCH, DEC_SEQ, PLE_DIM)),
        'g_mix': gain((DEPTH, D_MODEL)),
        'w_in': nrm((DEPTH, D_MODEL, N_IN), D_MODEL ** -0.5),
        'gdn_conv_w': nrm((DEPTH, GDN_CONV, CONV_CH), GDN_CONV ** -0.5),
        'gdn_A_log': a_log,
        'gdn_dt_bias': dt_bias,
        'gdn_norm_w': gain((DEPTH, GDN_DV)),
        'w_o_gdn': nrm((DEPTH, GDN_V_W, D_MODEL), GDN_V_W ** -0.5),
        'g_q_a': gain((DEPTH, Q_LORA)),
        'w_uq': nrm((DEPTH, Q_LORA, MLA_HEADS * (QK_NOPE + QK_ROPE)), Q_LORA ** -0.5),
        'g_kv_a': gain((DEPTH, KV_LORA)),
        'w_uk': nrm((DEPTH, KV_LORA, MLA_HEADS, QK_NOPE), KV_LORA ** -0.5),
        'w_uv': nrm((DEPTH, KV_LORA, MLA_HEADS, V_HEAD), KV_LORA ** -0.5),
        'w_o_mla': nrm((DEPTH, MLA_V_W, D_MODEL), MLA_V_W ** -0.5),
        'w_out': nrm((DEPTH, D_MODEL, D_MODEL), D_MODEL ** -0.5),
        'g_ffn': gain((DEPTH, D_MODEL)),
        'w_up': nrm((DEPTH, D_MODEL, 2 * D_FF), D_MODEL ** -0.5),
        'ffn_conv_w': nrm((DEPTH, FFN_CONV, 2 * D_FF), FFN_CONV ** -0.5),
        'w_down': nrm((DEPTH, D_FF, D_MODEL), D_FF ** -0.5),
        'g_ple': gain((DEPTH, D_MODEL)),
        'w_ple_gate': nrm((DEPTH, D_MODEL, D_MODEL), D_MODEL ** -0.5),
        'w_ple_proj': nrm((DEPTH, PLE_DIM, D_MODEL), PLE_DIM ** -0.5),
        'g_final': gain((D_MODEL,)),
    }


def reference(x_prompt, x_sample, cache_ckv, cache_krope, page_table, state_gdn, state_gdn_conv,
              state_ffn_conv, p_prompt, p_sample, g_mix, w_in, gdn_conv_w, gdn_A_log, gdn_dt_bias,
              gdn_norm_w, w_o_gdn, g_q_a, w_uq, g_kv_a, w_uk, w_uv, w_o_mla, w_out, g_ffn, w_up,
              ffn_conv_w, w_down, g_ple, w_ple_gate, w_ple_proj, g_final):
    bp, lp, _ = x_prompt.shape
    ts = x_sample.shape[1]
    past_len = page_table.shape[1] * cache_ckv.shape[2]
    cos_p, sin_p = rope_tables(jnp.arange(lp))
    cos_s, sin_s = rope_tables(past_len + jnp.arange(ts))
    xp, xs = x_prompt, x_sample
    ckv_p, kr_p, ckv_s, kr_s = [], [], [], []
    gdn_p, gdn_s, gconv_p, gconv_s, fconv_p, fconv_s = [], [], [], [], [], []
    for l in range(DEPTH):
        lw = dict(g_mix=g_mix[l], w_in=w_in[l], gdn_conv_w=gdn_conv_w[l], gdn_A_log=gdn_A_log[l],
                  gdn_dt_bias=gdn_dt_bias[l], gdn_norm_w=gdn_norm_w[l], w_o_gdn=w_o_gdn[l],
                  g_q_a=g_q_a[l], w_uq=w_uq[l], g_kv_a=g_kv_a[l], w_uk=w_uk[l], w_uv=w_uv[l],
                  w_o_mla=w_o_mla[l], w_out=w_out[l], g_ffn=g_ffn[l], w_up=w_up[l],
                  ffn_conv_w=ffn_conv_w[l], w_down=w_down[l], g_ple=g_ple[l],
                  w_ple_gate=w_ple_gate[l], w_ple_proj=w_ple_proj[l])
        xp, st = trunk_layer(
            xp, p_prompt[l], cos_p, sin_p, mla_attend_prompt,
            jnp.zeros((bp, GDN_CONV - 1, CONV_CH), xp.dtype),
            jnp.zeros((bp, GDN_HEADS, GDN_DK, GDN_DV), jnp.float32),
            jnp.zeros((bp, FFN_CONV - 1, 2 * D_FF), xp.dtype), lw)
        ckv_p.append(st[0].reshape(bp, lp // PAGE_SIZE, PAGE_SIZE, KV_LORA))
        kr_p.append(st[1].reshape(bp, lp // PAGE_SIZE, PAGE_SIZE, QK_ROPE))
        gdn_p.append(st[2])
        gconv_p.append(st[3])
        fconv_p.append(st[4])
        attend_s = functools.partial(mla_attend_sample, cache_ckv=cache_ckv, cache_kr=cache_krope,
                                     page_table=page_table, layer=l)
        xs, st = trunk_layer(xs, p_sample[l], cos_s, sin_s, attend_s,
                             state_gdn_conv[l], state_gdn[l], state_ffn_conv[l], lw)
        ckv_s.append(st[0])
        kr_s.append(st[1])
        gdn_s.append(st[2])
        gconv_s.append(st[3])
        fconv_s.append(st[4])
    y_prompt = rmsnorm(xp, g_final)
    y_sample = rmsnorm(xs, g_final)
    return (y_prompt, y_sample,
            jnp.stack(ckv_p), jnp.stack(kr_p), jnp.stack(ckv_s), jnp.stack(kr_s),
            jnp.stack(gdn_p), jnp.stack(gdn_s), jnp.stack(gconv_p), jnp.stack(gconv_s),
            jnp.stack(fconv_p), jnp.stack(fconv_s))
```

```python
import functools
import math

import jax
import jax.numpy as jnp
import numpy as np
from jax import lax
from jax.experimental import pallas as pl
from jax.experimental.pallas import tpu as pltpu

F32 = jnp.float32
BF16 = jnp.bfloat16
EPS = 1e-6
ROPE_BASE = 10000.0
HP = lax.Precision.HIGHEST
LANES = 128
SUBLANES = 8
NEG = -1e30
VMEM_LIMIT = 56 * 1024 * 1024

NT = (((1,), (1,)), ((), ()))
TN = (((0,), (0,)), ((), ()))


def _cp(sem):
    return pltpu.CompilerParams(dimension_semantics=sem, vmem_limit_bytes=VMEM_LIMIT)


def _sigmoid(x):
    return 1.0 / (1.0 + jnp.exp(-x))


def _silu(x):
    return x * _sigmoid(x)


def _bdot(a, b, dims=None):
    a = a.astype(BF16)
    b = b.astype(BF16)
    if dims is None:
        return jnp.dot(a, b, preferred_element_type=F32)
    return lax.dot_general(a, b, dims, preferred_element_type=F32)


def _rms_body(x_ref, g_ref, o_ref):
    x = x_ref[...]
    ms = jnp.mean(x * x, axis=-1, keepdims=True)
    o_ref[...] = (x * lax.rsqrt(ms + EPS) * g_ref[...]).astype(o_ref.dtype)


def _rmsnorm(x, g, out_dtype, tm):
    M, D = x.shape
    return pl.pallas_call(
        _rms_body,
        grid=(M // tm,),
        in_specs=[pl.BlockSpec((tm, D), lambda i: (i, 0)),
                  pl.BlockSpec((1, D), lambda i: (0, 0))],
        out_specs=pl.BlockSpec((tm, D), lambda i: (i, 0)),
        out_shape=jax.ShapeDtypeStruct((M, D), out_dtype),
        compiler_params=_cp(("parallel",)),
        name="rmsnorm",
    )(x, g.reshape(1, D))


def _rms_rows(x, g):
    return x * lax.rsqrt(jnp.mean(x * x, axis=-1, keepdims=True) + EPS) * g


def _mm_body(*refs, n_pairs, n_extras, epi, prenorm, a_as_extra):
    o_ref = refs[-1]
    a_raw = refs[0][...]
    a0 = _rms_rows(a_raw, refs[-2][...]).astype(BF16) if prenorm else a_raw
    accs = [jnp.dot(a0, refs[1][...], preferred_element_type=F32)]
    accs += [jnp.dot(refs[2 * p][...], refs[2 * p + 1][...], preferred_element_type=F32)
             for p in range(1, n_pairs)]
    extras = [a_raw] if a_as_extra else []
    extras += [r[...] for r in refs[2 * n_pairs:2 * n_pairs + n_extras]]
    o_ref[...] = epi(accs, extras).astype(o_ref.dtype)


def _matmul(pairs, extras, epi, out_dtype, tm, tn, name, prenorm=None, rowvecs=(),
            a_as_extra=False):
    M = pairs[0][0].shape[0]
    N = pairs[0][1].shape[1]
    in_specs, args = [], []
    for a, w in pairs:
        K = a.shape[1]
        in_specs += [pl.BlockSpec((tm, K), lambda j, i: (i, 0)),
                     pl.BlockSpec((K, tn), lambda j, i: (0, j))]
        args += [a, w]
    for e, off in extras:
        in_specs.append(pl.BlockSpec((tm, tn), lambda j, i, off=off: (i, off + j)))
        args.append(e)
    for v in rowvecs:
        in_specs.append(pl.BlockSpec((1, tn), lambda j, i: (0, j)))
        args.append(v.reshape(1, N))
    if prenorm is not None:
        K0 = pairs[0][0].shape[1]
        in_specs.append(pl.BlockSpec((1, K0), lambda j, i: (0, 0)))
        args.append(prenorm.reshape(1, K0))
    return pl.pallas_call(
        functools.partial(_mm_body, n_pairs=len(pairs), n_extras=len(extras) + len(rowvecs),
                          epi=epi, prenorm=prenorm is not None, a_as_extra=a_as_extra),
        grid=(N // tn, M // tm),
        in_specs=in_specs,
        out_specs=pl.BlockSpec((tm, tn), lambda j, i: (i, j)),
        out_shape=jax.ShapeDtypeStruct((M, N), out_dtype),
        compiler_params=_cp(("parallel", "parallel")),
        name=name,
    )(*args)


def _epi_plain(accs, extras):
    return accs[0]


def _epi_merge(accs, extras):
    return _sigmoid(extras[0]) * accs[0] + _sigmoid(extras[1]) * accs[1]


def _epi_resid(accs, extras):
    return extras[0] + accs[0]


def _epi_ple(accs, extras):
    return extras[0] + _sigmoid(accs[0]) * accs[1]


def _epi_ple_final(accs, extras):
    return _rms_rows(extras[0] + _sigmoid(accs[0]) * accs[1], extras[1])


def _headmm_body(a_ref, w_ref, o_ref):
    o_ref[...] = jnp.dot(a_ref[...], w_ref[0], preferred_element_type=F32).astype(o_ref.dtype)


def _head_matmul(a, w, out_dtype, name):
    M = a.shape[0]
    H, K, N = w.shape
    return pl.pallas_call(
        _headmm_body,
        grid=(H,),
        in_specs=[pl.BlockSpec((M, K), lambda h: (0, h)),
                  pl.BlockSpec((1, K, N), lambda h: (h, 0, 0))],
        out_specs=pl.BlockSpec((M, N), lambda h: (0, h)),
        out_shape=jax.ShapeDtypeStruct((M, H * N), out_dtype),
        compiler_params=_cp(("parallel",)),
        name=name,
    )(a, w)


def _gdn_body(cin_ref, z_ref, sm_ref, cw_ref, cs_ref, s0_ref, al_ref, dtb_ref, nw_ref,
              o_ref, sout_ref, xp, S, smp, *, NB, C, Lb, H, DK, DV, KC):
    c = pl.program_id(1)
    last = pl.num_programs(1) - 1
    HIST = KC - 1
    R0 = SUBLANES

    @pl.when(c == 0)
    def _init():
        xp[...] = jnp.zeros_like(xp)
        for n in range(NB):
            xp[n, R0 - HIST:R0, :] = cs_ref[n]
        S[...] = s0_ref[...]
        smp[...] = jnp.zeros_like(smp)

    for n in range(NB):
        xp[n, R0:R0 + Lb, :] = cin_ref[n]
        smp[n, 0:Lb, :] = sm_ref[n]

    lane = lax.broadcasted_iota(jnp.int32, (C, LANES), 1)
    row = lax.broadcasted_iota(jnp.int32, (C, LANES), 0)
    valid = row < Lb
    ri = lax.broadcasted_iota(jnp.int32, (C, C), 0)
    ci = lax.broadcasted_iota(jnp.int32, (C, C), 1)
    tril = ri >= ci
    strict = ri > ci
    eye = (ri == ci).astype(F32)
    trilf = tril.astype(F32)
    sel = (lax.broadcasted_iota(jnp.int32, (SUBLANES, LANES), 1)
           == lax.broadcasted_iota(jnp.int32, (SUBLANES, LANES), 0) + H).astype(F32)

    gates = []
    for n in range(NB):
        smv = smp[n]
        beta = jnp.where(valid & (lane < H), _sigmoid(smv), 0.0)
        xa = smv + dtb_ref[...]
        softplus = jnp.maximum(xa, 0.0) + jnp.log(1.0 + jnp.exp(-jnp.abs(xa)))
        g = jnp.where(valid & (lane >= H) & (lane < 2 * H), -jnp.exp(al_ref[...]) * softplus, 0.0)
        gc = jnp.dot(trilf, g, precision=HP, preferred_element_type=F32)
        gr = lax.dot_general(sel, gc, NT, precision=HP, preferred_element_type=F32)
        gates.append((beta, gc, gr))

    def conv(n, off):
        acc = xp[n, R0 - HIST:R0 - HIST + C, off:off + LANES] * cw_ref[0:1, off:off + LANES]
        for j in range(1, KC):
            acc = acc + (xp[n, R0 - HIST + j:R0 - HIST + j + C, off:off + LANES]
                         * cw_ref[j:j + 1, off:off + LANES])
        return _silu(acc)

    chains = [(n, h) for n in range(NB) for h in range(H)]
    st = []
    for n, h in chains:
        beta, gc, gr = gates[n]
        q = conv(n, h * DK)
        k = conv(n, H * DK + h * DK)
        v = conv(n, 2 * H * DK + h * DV)
        q = q * lax.rsqrt(jnp.sum(q * q, axis=-1, keepdims=True) + EPS) * (DK ** -0.5)
        k = k * lax.rsqrt(jnp.sum(k * k, axis=-1, keepdims=True) + EPS)
        gcol = gc[:, H + h:H + h + 1]
        grow = gr[h:h + 1, :]
        bcol = beta[:, h:h + 1]
        glast = gc[C - 1:C, H + h:H + h + 1]
        decay = jnp.where(tril, jnp.exp(jnp.where(tril, gcol - grow, 0.0)), 0.0)
        kb = k * bcol
        egc = jnp.exp(gcol)
        st.append(dict(q=q, k=k, gcol=gcol, glast=glast, decay=decay, kb=kb,
                       rhs=jnp.concatenate([v * bcol, kb * egc], axis=1), qe=q * egc))

    for s in st:
        s["pw"] = -jnp.where(strict, _bdot(s["kb"], s["k"], NT) * s["decay"], 0.0)
        s["tinv"] = eye + s["pw"]
    n_levels = int(math.ceil(math.log2(C)))
    for s in st:
        s["pw"] = _bdot(s["pw"], s["pw"])
    for _ in range(n_levels - 2):
        for s in st:
            both = _bdot(jnp.concatenate([s["pw"], s["tinv"]], axis=0), s["pw"])
            s["pw"] = both[0:C]
            s["tinv"] = s["tinv"] + both[C:2 * C]
    for s in st:
        s["tinv"] = s["tinv"] + _bdot(s["tinv"], s["pw"])

    for s in st:
        sol = _bdot(s["tinv"], s["rhs"])
        s["u"] = sol[:, 0:DV]
        s["w"] = sol[:, DV:DV + DK]
        s["qk"] = _bdot(s["q"], s["k"], NT) * s["decay"]

    for (n, h), s in zip(chains, st):
        s["S"] = S[n, h]
        ws = _bdot(jnp.concatenate([s["w"], s["qe"]], axis=0), s["S"])
        s["v_new"] = s["u"] - ws[0:C]
        s["o"] = ws[C:2 * C]

    for (n, h), s in zip(chains, st):
        o = s["o"] + _bdot(s["qk"], s["v_new"])
        S[n, h] = (s["S"] * jnp.exp(s["glast"])
                   + _bdot(s["k"] * jnp.exp(s["glast"] - s["gcol"]), s["v_new"], TN))
        on = o * lax.rsqrt(jnp.mean(o * o, axis=-1, keepdims=True) + EPS) * nw_ref[...]
        zz = z_ref[n, :, h * DV:(h + 1) * DV]
        o_ref[n, :, h * DV:(h + 1) * DV] = (on[0:Lb] * _silu(zz)).astype(o_ref.dtype)

    if Lb == C:
        for n in range(NB):
            xp[n, R0 - HIST:R0, :] = xp[n, R0 + C - HIST:R0 + C, :]

    @pl.when(c == last)
    def _fin():
        sout_ref[...] = S[...]


def _gdn(main3, small3, conv_w, conv_state, s0, a_log, dt_bias, norm_w, *, NB, C, Lb, H, DK, DV,
         out_dtype):
    B, L, _ = main3.shape
    KC, CW = conv_w.shape
    nc = L // Lb
    assert nc == 1 or Lb == C
    assert B % NB == 0
    pad = lambda vec, off: jnp.zeros((1, LANES), F32).at[0, off:off + H].set(vec.astype(F32))
    body = functools.partial(_gdn_body, NB=NB, C=C, Lb=Lb, H=H, DK=DK, DV=DV, KC=KC)
    return pl.pallas_call(
        body,
        grid=(B // NB, nc),
        in_specs=[
            pl.BlockSpec((NB, Lb, CW), lambda b, c: (b, c, 0)),
            pl.BlockSpec((NB, Lb, H * DV), lambda b, c: (b, c, CW // (H * DV))),
            pl.BlockSpec((NB, Lb, LANES), lambda b, c: (b, c, 2)),
            pl.BlockSpec((KC, CW), lambda b, c: (0, 0)),
            pl.BlockSpec((NB, KC - 1, CW), lambda b, c: (b, 0, 0)),
            pl.BlockSpec((NB, H, DK, DV), lambda b, c: (b, 0, 0, 0)),
            pl.BlockSpec((1, LANES), lambda b, c: (0, 0)),
            pl.BlockSpec((1, LANES), lambda b, c: (0, 0)),
            pl.BlockSpec((1, DV), lambda b, c: (0, 0)),
        ],
        out_specs=[
            pl.BlockSpec((NB, Lb, H * DV), lambda b, c: (b, c, 0)),
            pl.BlockSpec((NB, H, DK, DV), lambda b, c: (b, 0, 0, 0)),
        ],
        out_shape=[jax.ShapeDtypeStruct((B, L, H * DV), out_dtype),
                   jax.ShapeDtypeStruct((B, H, DK, DV), F32)],
        scratch_shapes=[pltpu.VMEM((NB, SUBLANES + C, CW), F32),
                        pltpu.VMEM((NB, H, DK, DV), F32),
                        pltpu.VMEM((NB, C, LANES), F32)],
        compiler_params=_cp(("parallel", "arbitrary")),
        name="gdn",
    )(main3, main3, small3, conv_w, conv_state, s0, pad(a_log, H), pad(dt_bias, H),
      norm_w.reshape(1, DV).astype(F32))


def _mlaproj_body(cq_ref, ckv_ref, sm_ref, cos_ref, sin_ref, wq_ref, gq_ref, gkv_ref,
                  qn_ref, qr_ref, ckv_o_ref, ckvb_ref, kr_o_ref, krb_ref, *, H, NOPE, ROPE, scale):
    cq = cq_ref[...]
    cq = (cq * lax.rsqrt(jnp.mean(cq * cq, axis=-1, keepdims=True) + EPS) * gq_ref[...]).astype(BF16)
    q = jnp.dot(cq, wq_ref[...], preferred_element_type=F32)
    HW = H * LANES
    cosp = cos_ref[...]
    sinp = sin_ref[...]
    cos_h = jnp.concatenate([cosp] * H, axis=1)
    sin_h = jnp.concatenate([sinp] * H, axis=1)
    qn_ref[...] = (q[:, 0:H * NOPE] * scale).astype(qn_ref.dtype)
    qr = q[:, H * NOPE:H * NOPE + HW] * cos_h + q[:, H * NOPE + HW:H * NOPE + 2 * HW] * sin_h
    qr_ref[...] = (qr * scale).astype(qr_ref.dtype)
    ckv = ckv_ref[...]
    ckv = ckv * lax.rsqrt(jnp.mean(ckv * ckv, axis=-1, keepdims=True) + EPS) * gkv_ref[...]
    ckv_o_ref[...] = ckv
    ckvb_ref[...] = ckv.astype(BF16)
    sm = sm_ref[...]
    kr = sm[:, 0:LANES] * cosp + sm[:, LANES:2 * LANES] * sinp
    kr_o_ref[...] = kr[:, 0:ROPE]
    krb_ref[...] = kr.astype(BF16)


def _mla_proj(main2, small2, cosp, sinp, wq, g_q, g_kv, *, tl, H, NOPE, ROPE, QL, KVL, scale, table_blocks):
    T, W = main2.shape
    cq_blk = (W - QL - KVL) // QL
    assert QL == KVL
    nt = table_blocks
    body = functools.partial(_mlaproj_body, H=H, NOPE=NOPE, ROPE=ROPE, scale=scale)
    return pl.pallas_call(
        body,
        grid=(T // tl,),
        in_specs=[
            pl.BlockSpec((tl, QL), lambda i: (i, cq_blk)),
            pl.BlockSpec((tl, KVL), lambda i: (i, cq_blk + 1)),
            pl.BlockSpec((tl, 2 * LANES), lambda i: (i, 0)),
            pl.BlockSpec((tl, LANES), lambda i: (i % nt, 0)),
            pl.BlockSpec((tl, LANES), lambda i: (i % nt, 0)),
            pl.BlockSpec(wq.shape, lambda i: (0, 0)),
            pl.BlockSpec((1, QL), lambda i: (0, 0)),
            pl.BlockSpec((1, KVL), lambda i: (0, 0)),
        ],
        out_specs=[
            pl.BlockSpec((tl, H * NOPE), lambda i: (i, 0)),
            pl.BlockSpec((tl, H * LANES), lambda i: (i, 0)),
            pl.BlockSpec((tl, KVL), lambda i: (i, 0)),
            pl.BlockSpec((tl, KVL), lambda i: (i, 0)),
            pl.BlockSpec((tl, ROPE), lambda i: (i, 0)),
            pl.BlockSpec((tl, LANES), lambda i: (i, 0)),
        ],
        out_shape=[
            jax.ShapeDtypeStruct((T, H * NOPE), BF16),
            jax.ShapeDtypeStruct((T, H * LANES), BF16),
            jax.ShapeDtypeStruct((T, KVL), F32),
            jax.ShapeDtypeStruct((T, KVL), BF16),
            jax.ShapeDtypeStruct((T, ROPE), F32),
            jax.ShapeDtypeStruct((T, LANES), BF16),
        ],
        compiler_params=_cp(("parallel",)),
        name="mla_proj",
    )(main2, main2, small2, cosp, sinp, wq, g_q.reshape(1, QL), g_kv.reshape(1, KVL))


def _pattn_body(qn_ref, qr_ref, k_ref, kr_ref, wuk_ref, wuv_ref, o_ref,
                ql_s, qr_s, m_s, l_s, acc_s, *, H, tq, tk, NOPE, VH):
    i = pl.program_id(1)
    for h in range(H):
        ql_s[h * tq:(h + 1) * tq, :] = jnp.dot(
            qn_ref[0, :, h * NOPE:(h + 1) * NOPE], wuk_ref[h], preferred_element_type=F32).astype(BF16)
        qr_s[h * tq:(h + 1) * tq, :] = qr_ref[0, :, h * LANES:(h + 1) * LANES]
    m_s[...] = jnp.full_like(m_s, NEG)
    l_s[...] = jnp.zeros_like(l_s)
    acc_s[...] = jnp.zeros_like(acc_s)
    R = H * tq
    qpos = i * tq + (lax.broadcasted_iota(jnp.int32, (R, tk), 0) & (tq - 1))
    kofs = lax.broadcasted_iota(jnp.int32, (R, tk), 1)
    nkv = (i * tq + tq + tk - 1) // tk

    def scores(j):
        start = pl.multiple_of(j * tk, tk)
        return (lax.dot_general(ql_s[...], k_ref[0, pl.ds(start, tk), :], NT,
                                preferred_element_type=F32)
                + lax.dot_general(qr_s[...], kr_ref[0, pl.ds(start, tk), :], NT,
                                  preferred_element_type=F32))

    def absorb(s, j, masked):
        start = pl.multiple_of(j * tk, tk)
        if masked:
            s = jnp.where(kofs + j * tk <= qpos, s, NEG)
        m_prev = m_s[...]
        m_new = jnp.maximum(m_prev, jnp.max(s, axis=-1, keepdims=True))
        alpha = jnp.exp(m_prev - m_new)
        p = jnp.exp(s - m_new)
        l_s[...] = alpha * l_s[...] + jnp.sum(p, axis=-1, keepdims=True)
        acc_s[...] = alpha * acc_s[...] + jnp.dot(p.astype(BF16), k_ref[0, pl.ds(start, tk), :],
                                                  preferred_element_type=F32)
        m_s[...] = m_new

    def full_chunk(j, s_cur):
        s_next = scores(j + 1)
        absorb(s_cur, j, False)
        return s_next

    s_last = lax.fori_loop(0, nkv - 1, full_chunk, scores(0))
    absorb(s_last, nkv - 1, True)
    o = (acc_s[...] / l_s[...]).astype(BF16)
    for h in range(H):
        o_ref[0, :, h * VH:(h + 1) * VH] = jnp.dot(
            o[h * tq:(h + 1) * tq, :], wuv_ref[h], preferred_element_type=F32).astype(o_ref.dtype)


def _prompt_attention(qn, qr, ckvb, krb, wuk_t, wuv, *, tq, tk):
    B, L, _ = qn.shape
    H, NOPE, KVL = wuk_t.shape
    VH = wuv.shape[2]
    assert tq & (tq - 1) == 0 and tk % tq == 0 and H % 2 == 0
    body = functools.partial(_pattn_body, H=H, tq=tq, tk=tk, NOPE=NOPE, VH=VH)
    return pl.pallas_call(
        body,
        grid=(B, L // tq),
        in_specs=[
            pl.BlockSpec((1, tq, H * NOPE), lambda b, i: (b, i, 0)),
            pl.BlockSpec((1, tq, H * LANES), lambda b, i: (b, i, 0)),
            pl.BlockSpec((1, L, KVL), lambda b, i: (b, 0, 0)),
            pl.BlockSpec((1, L, LANES), lambda b, i: (b, 0, 0)),
            pl.BlockSpec((H, NOPE, KVL), lambda b, i: (0, 0, 0)),
            pl.BlockSpec((H, KVL, VH), lambda b, i: (0, 0, 0)),
        ],
        out_specs=pl.BlockSpec((1, tq, H * VH), lambda b, i: (b, i, 0)),
        out_shape=jax.ShapeDtypeStruct((B, L, H * VH), BF16),
        scratch_shapes=[pltpu.VMEM((H * tq, KVL), BF16),
                        pltpu.VMEM((H * tq, LANES), BF16),
                        pltpu.VMEM((H * tq, 1), F32),
                        pltpu.VMEM((H * tq, 1), F32),
                        pltpu.VMEM((H * tq, KVL), F32)],
        compiler_params=_cp(("parallel", "arbitrary")),
        name="prompt_attn",
    )(qn, qr, ckvb, krb, wuk_t, wuv)


def _sattn_body(pt_ref, ql_ref, qr_ref, cn_ref, krn_ref, cc_hbm, ckt_hbm, o_ref,
                kbuf, rbuf, sem, m_s, l_s, acc_s, knew, rnew, *, P, NC, PS, ROPE, H, T, layer):
    b = pl.program_id(0)
    c = pl.program_id(1)
    step = b * NC + c
    nsteps = pl.num_programs(0) * NC
    slot = step % 2

    def page_copies(stp, sl):
        bb = stp // NC
        cc = stp % NC
        cps = []
        for p in range(P):
            pg = pt_ref[bb, cc * P + p]
            cps.append(pltpu.make_async_copy(cc_hbm.at[layer, pg], kbuf.at[sl, p], sem.at[0, sl]))
            cps.append(pltpu.make_async_copy(ckt_hbm.at[layer, pg], rbuf.at[sl, p], sem.at[1, sl]))
        return cps

    @pl.when(step == 0)
    def _first():
        for cp in page_copies(0, 0):
            cp.start()

    @pl.when(step + 1 < nsteps)
    def _prefetch():
        for cp in page_copies(step + 1, 1 - slot):
            cp.start()

    for cp in page_copies(step, slot):
        cp.wait()

    @pl.when(c == 0)
    def _init():
        m_s[...] = jnp.full_like(m_s, NEG)
        l_s[...] = jnp.zeros_like(l_s)
        acc_s[...] = jnp.zeros_like(acc_s)

    ql = ql_ref[0]
    qr = qr_ref[0]

    def update(s, vals):
        m_prev = m_s[...]
        m_new = jnp.maximum(m_prev, jnp.max(s, axis=-1, keepdims=True))
        alpha = jnp.exp(m_prev - m_new)
        p = jnp.exp(s - m_new)
        l_s[...] = alpha * l_s[...] + jnp.sum(p, axis=-1, keepdims=True)
        acc_s[...] = alpha * acc_s[...] + jnp.dot(p.astype(BF16), vals, preferred_element_type=F32)
        m_s[...] = m_new

    k = kbuf[slot].reshape(P * PS, kbuf.shape[-1]).astype(BF16)
    qr_rope = qr[:, 0:ROPE]
    s_rope = jnp.concatenate(
        [jnp.dot(qr_rope, rbuf[slot, p].astype(BF16), preferred_element_type=F32) for p in range(P)],
        axis=1)
    s = lax.dot_general(ql, k, NT, preferred_element_type=F32) + s_rope
    update(s, k)

    @pl.when(c == NC - 1)
    def _fin():
        knew[...] = jnp.zeros_like(knew)
        rnew[...] = jnp.zeros_like(rnew)
        knew[0:T, :] = cn_ref[0]
        rnew[0:T, :] = krn_ref[0]
        kn = knew[...].astype(BF16)
        rn = rnew[...].astype(BF16)
        s2 = (lax.dot_general(ql, kn, NT, preferred_element_type=F32)
              + lax.dot_general(qr, rn, NT, preferred_element_type=F32))
        R = T * H
        trow = lax.broadcasted_iota(jnp.int32, (R, LANES), 0) // H
        kcol = lax.broadcasted_iota(jnp.int32, (R, LANES), 1)
        s2 = jnp.where(kcol <= trow, s2, NEG)
        update(s2, kn)
        o_ref[0] = acc_s[...] / l_s[...]


def _sample_attention(page_table, ql, qr, cn, krn, cache_ckv, cache_krt, *, P, H, T, layer):
    DB, NPG = page_table.shape
    _, _, PS, KVL = cache_ckv.shape
    ROPE = cache_krt.shape[2]
    NC = NPG // P
    R = T * H
    body = functools.partial(_sattn_body, P=P, NC=NC, PS=PS, ROPE=ROPE, H=H, T=T, layer=layer)
    grid_spec = pltpu.PrefetchScalarGridSpec(
        num_scalar_prefetch=1,
        grid=(DB, NC),
        in_specs=[
            pl.BlockSpec((1, R, KVL), lambda b, c, pt: (b, 0, 0)),
            pl.BlockSpec((1, R, LANES), lambda b, c, pt: (b, 0, 0)),
            pl.BlockSpec((1, T, KVL), lambda b, c, pt: (b, 0, 0)),
            pl.BlockSpec((1, T, LANES), lambda b, c, pt: (b, 0, 0)),
            pl.BlockSpec(memory_space=pl.ANY),
            pl.BlockSpec(memory_space=pl.ANY),
        ],
        out_specs=pl.BlockSpec((1, R, KVL), lambda b, c, pt: (b, 0, 0)),
        scratch_shapes=[pltpu.VMEM((2, P, PS, KVL), F32),
                        pltpu.VMEM((2, P, ROPE, PS), F32),
                        pltpu.SemaphoreType.DMA((2, 2)),
                        pltpu.VMEM((R, 1), F32),
                        pltpu.VMEM((R, 1), F32),
                        pltpu.VMEM((R, KVL), F32),
                        pltpu.VMEM((LANES, KVL), F32),
                        pltpu.VMEM((LANES, LANES), F32)],
    )
    return pl.pallas_call(
        body,
        grid_spec=grid_spec,
        out_shape=jax.ShapeDtypeStruct((DB, R, KVL), F32),
        compiler_params=_cp(("arbitrary", "arbitrary")),
        name="sample_attn",
    )(page_table, ql, qr, cn, krn, cache_ckv, cache_krt)


def _upconv_body(a_ref, wg_ref, wv_ref, cwg_ref, cwv_ref, sg_ref, sv_ref,
                 act_ref, ng_ref, nv_ref, xg, xv, *, tm, KC, tiles_per_seq):
    i = pl.program_id(1)
    HIST = KC - 1
    R0 = SUBLANES

    @pl.when(i % tiles_per_seq == 0)
    def _init():
        xg[R0 - HIST:R0, :] = sg_ref[0]
        xv[R0 - HIST:R0, :] = sv_ref[0]

    a = a_ref[...]
    xg[R0:R0 + tm, :] = jnp.dot(a, wg_ref[...], preferred_element_type=F32)
    xv[R0:R0 + tm, :] = jnp.dot(a, wv_ref[...], preferred_element_type=F32)

    def conv(x, w_ref):
        acc = x[R0 - HIST:R0 - HIST + tm, :] * w_ref[0:1, :]
        for j in range(1, KC):
            acc = acc + x[R0 - HIST + j:R0 - HIST + j + tm, :] * w_ref[j:j + 1, :]
        return acc

    act_ref[...] = (_silu(conv(xg, cwg_ref)) * conv(xv, cwv_ref)).astype(act_ref.dtype)
    tail_g = xg[R0 + tm - HIST:R0 + tm, :]
    tail_v = xv[R0 + tm - HIST:R0 + tm, :]
    ng_ref[0] = tail_g
    nv_ref[0] = tail_v
    xg[R0 - HIST:R0, :] = tail_g
    xv[R0 - HIST:R0, :] = tail_v


def _up_conv_act(h, w_up, conv_w, state, *, L, tm, tn):
    T, D = h.shape
    F2 = w_up.shape[1]
    F = F2 // 2
    B = T // L
    KC = conv_w.shape[0]
    nv = F // tn
    tps = L // tm
    body = functools.partial(_upconv_body, tm=tm, KC=KC, tiles_per_seq=tps)
    act, ng, nvv = pl.pallas_call(
        body,
        grid=(F // tn, T // tm),
        in_specs=[
            pl.BlockSpec((tm, D), lambda j, i: (i, 0)),
            pl.BlockSpec((D, tn), lambda j, i: (0, j)),
            pl.BlockSpec((D, tn), lambda j, i: (0, j + nv)),
            pl.BlockSpec((KC, tn), lambda j, i: (0, j)),
            pl.BlockSpec((KC, tn), lambda j, i: (0, j + nv)),
            pl.BlockSpec((1, KC - 1, tn), lambda j, i: (i // tps, 0, j)),
            pl.BlockSpec((1, KC - 1, tn), lambda j, i: (i // tps, 0, j + nv)),
        ],
        out_specs=[
            pl.BlockSpec((tm, tn), lambda j, i: (i, j)),
            pl.BlockSpec((1, KC - 1, tn), lambda j, i: (i // tps, 0, j)),
            pl.BlockSpec((1, KC - 1, tn), lambda j, i: (i // tps, 0, j)),
        ],
        out_shape=[jax.ShapeDtypeStruct((T, F), BF16),
                   jax.ShapeDtypeStruct((B, KC - 1, F), F32),
                   jax.ShapeDtypeStruct((B, KC - 1, F), F32)],
        scratch_shapes=[pltpu.VMEM((SUBLANES + tm, tn), F32), pltpu.VMEM((SUBLANES + tm, tn), F32)],
        compiler_params=_cp(("parallel", "arbitrary")),
        name="up_conv_act",
    )(h, w_up, w_up, conv_w, conv_w, state, state)
    return act, jnp.concatenate([ng, nvv], axis=2)


def _ffnconv_tm_body(xg_ref, xv_ref, wg_ref, wv_ref, o_ref, *, T, KC):
    for t in range(T):
        g = xg_ref[t] * wg_ref[0:1, :]
        v = xv_ref[t] * wv_ref[0:1, :]
        for j in range(1, KC):
            g = g + xg_ref[t + j] * wg_ref[j:j + 1, :]
            v = v + xv_ref[t + j] * wv_ref[j:j + 1, :]
        o_ref[t] = (_silu(g) * v).astype(o_ref.dtype)


def _ffn_conv_timemajor(xp, conv_w, *, tc):
    TT, NB, F2 = xp.shape
    F = F2 // 2
    KC = conv_w.shape[0]
    T = TT - (KC - 1)
    nv = F // tc
    body = functools.partial(_ffnconv_tm_body, T=T, KC=KC)
    return pl.pallas_call(
        body,
        grid=(F // tc,),
        in_specs=[
            pl.BlockSpec((TT, NB, tc), lambda j: (0, 0, j)),
            pl.BlockSpec((TT, NB, tc), lambda j: (0, 0, j + nv)),
            pl.BlockSpec((KC, tc), lambda j: (0, j)),
            pl.BlockSpec((KC, tc), lambda j: (0, j + nv)),
        ],
        out_specs=pl.BlockSpec((T, NB, tc), lambda j: (0, 0, j)),
        out_shape=jax.ShapeDtypeStruct((T, NB, F), BF16),
        compiler_params=_cp(("parallel",)),
        name="ffn_conv_tm",
    )(xp, xp, conv_w, conv_w)


def _prep_weights(w_in, w_uq, w_uk, w_uv, *, H, DK, DV, QL, KVL, NOPE, ROPE, VH, D):
    conv_ch = 2 * H * DK + H * DV
    sizes = (conv_ch, H * DV, H, H, QL, KVL, ROPE, D, D)
    offs = np.cumsum((0,) + sizes)
    seg = lambda n: w_in[:, offs[n]:offs[n + 1]]
    w_conv, w_z, w_b, w_a, w_cq, w_ckv, w_kr, w_ga, w_gb = (seg(n) for n in range(9))
    half = ROPE // 2
    w_main = jnp.concatenate([w_conv, w_z, w_ga, w_gb, w_cq, w_ckv], axis=1).astype(BF16)
    zc = lambda n: jnp.zeros((D, n), w_in.dtype)
    w_kr_sw = jnp.concatenate([w_kr[:, half:], w_kr[:, :half]], axis=1)
    w_small = jnp.concatenate([w_kr, zc(LANES - ROPE), w_kr_sw, zc(LANES - ROPE),
                               w_b, w_a, zc(LANES - 2 * H)], axis=1).astype(BF16)
    wq = w_uq.reshape(QL, H, NOPE + ROPE)
    wq_n = wq[:, :, :NOPE].reshape(QL, H * NOPE)
    wq_r = wq[:, :, NOPE:]
    zr = jnp.zeros((QL, H, LANES - ROPE), w_uq.dtype)
    wq_rp = jnp.concatenate([wq_r, zr], axis=2).reshape(QL, H * LANES)
    wq_rs = jnp.concatenate([wq_r[:, :, half:], wq_r[:, :, :half], zr], axis=2).reshape(QL, H * LANES)
    wq_all = jnp.concatenate([wq_n, wq_rp, wq_rs], axis=1).astype(BF16)
    wuk_t = jnp.transpose(w_uk, (1, 2, 0)).astype(BF16)
    wuv = jnp.transpose(w_uv, (1, 0, 2)).astype(BF16)
    return w_main, w_small, wq_all, wuk_t, wuv


def _rope_tables(pos, rope):
    half = rope // 2
    inv = np.exp(-math.log(ROPE_BASE) * np.arange(half, dtype=np.float64) / half)
    ang = np.asarray(pos).astype(np.float64)[:, None] * inv[None, :]
    c, s = np.cos(ang).astype(np.float32), np.sin(ang).astype(np.float32)
    z = np.zeros((ang.shape[0], LANES - rope), np.float32)
    return np.concatenate([c, c, z], axis=1), np.concatenate([-s, s, z], axis=1)


def _last_rows(hist, x):
    n, L = hist.shape[1], x.shape[1]
    if L >= n:
        return x[:, L - n:]
    return jnp.concatenate([hist[:, L:], x], axis=1)


def kernel(x_prompt, x_sample, cache_ckv, cache_krope, page_table, state_gdn, state_gdn_conv,
           state_ffn_conv, p_prompt, p_sample, g_mix, w_in, gdn_conv_w, gdn_A_log, gdn_dt_bias,
           gdn_norm_w, w_o_gdn, g_q_a, w_uq, g_kv_a, w_uk, w_uv, w_o_mla, w_out, g_ffn, w_up,
           ffn_conv_w, w_down, g_ple, w_ple_gate, w_ple_proj, g_final):
    BP, LP, D = x_prompt.shape
    DB, TS, _ = x_sample.shape
    depth = w_in.shape[0]
    H, DK, DV = state_gdn.shape[2], state_gdn.shape[3], state_gdn.shape[4]
    KC = gdn_conv_w.shape[1]
    CW = gdn_conv_w.shape[2]
    QL = g_q_a.shape[1]
    KVL, MH, NOPE = w_uk.shape[1], w_uk.shape[2], w_uk.shape[3]
    VH = w_uv.shape[3]
    ROPE = cache_krope.shape[3]
    PS = cache_ckv.shape[2]
    NPG = page_table.shape[1]
    F2 = w_up.shape[2]
    F = F2 // 2
    FK = ffn_conv_w.shape[1]
    assert H == MH
    scale = (NOPE + ROPE) ** -0.5
    past_len = NPG * PS
    W_MAIN = CW + H * DV + 2 * D + QL + KVL

    cos_p, sin_p = _rope_tables(np.arange(LP), ROPE)
    cos_s, sin_s = _rope_tables(past_len + np.arange(TS), ROPE)
    cos_s = np.tile(cos_s, (DB, 1))
    sin_s = np.tile(sin_s, (DB, 1))
    cache_krt = jnp.swapaxes(cache_krope, 2, 3)

    xp = x_prompt.reshape(BP * LP, D)
    xs = x_sample.reshape(DB * TS, D)
    outs = {k: [] for k in ("ckv_p", "kr_p", "ckv_s", "kr_s", "gdn_p", "gdn_s",
                            "gc_p", "gc_s", "fc_p", "fc_s")}

    for l in range(depth):
        w_main, w_small, wq_all, wuk_t, wuv = _prep_weights(
            w_in[l], w_uq[l], w_uk[l], w_uv[l], H=H, DK=DK, DV=DV, QL=QL, KVL=KVL,
            NOPE=NOPE, ROPE=ROPE, VH=VH, D=D)
        wo_gdn = w_o_gdn[l].astype(BF16)
        wo_mla = w_o_mla[l].astype(BF16)
        wout = w_out[l].astype(BF16)
        wup = w_up[l].astype(BF16)
        wdown = w_down[l].astype(BF16)
        wpg = w_ple_gate[l].astype(BF16)
        wpp = w_ple_proj[l].astype(BF16)

        def layer(x, p_l, *, prompt):
            T = x.shape[0]
            tr = min(512, T)
            tm = min(1024, T)
            tn = 1024
            h = _rmsnorm(x, g_mix[l], BF16, tr)
            main = _matmul([(h, w_main)], [], _epi_plain, F32, tm, tn, "in_proj")
            small = _matmul([(h, w_small)], [], _epi_plain, F32, tr, 3 * LANES, "in_proj_small")
            if prompt:
                nb, L = BP, LP
                conv_state = jnp.zeros((nb, KC - 1, CW), F32)
                s0 = jnp.zeros((nb, H, DK, DV), F32)
                cos_t, sin_t, chunk = cos_p, sin_p, 64
            else:
                nb, L = DB, TS
                conv_state = state_gdn_conv[l]
                s0 = state_gdn[l]
                cos_t, sin_t, chunk = cos_s, sin_s, SUBLANES
            tl = min(256, T)
            tblk = cos_t.shape[0] // tl
            main3 = main.reshape(nb, L, W_MAIN)
            small3 = small.reshape(nb, L, 3 * LANES)
            o_gdn, s_new = _gdn(main3, small3, gdn_conv_w[l], conv_state, s0, gdn_A_log[l],
                                gdn_dt_bias[l], gdn_norm_w[l], NB=2, C=chunk, Lb=min(chunk, L),
                                H=H, DK=DK, DV=DV, out_dtype=BF16 if prompt else F32)
            o_gdn = o_gdn.reshape(T, H * DV).astype(BF16)
            new_gconv = _last_rows(conv_state, main3[:, :, :CW])

            qn, qr, ckv, ckvb, kr, krb = _mla_proj(
                main, small, jnp.asarray(cos_t), jnp.asarray(sin_t), wq_all, g_q_a[l], g_kv_a[l],
                tl=tl, H=H, NOPE=NOPE, ROPE=ROPE, QL=QL, KVL=KVL, scale=scale, table_blocks=tblk)
            if prompt:
                o_mla = _prompt_attention(
                    qn.reshape(nb, L, H * NOPE), qr.reshape(nb, L, H * LANES),
                    ckvb.reshape(nb, L, KVL), krb.reshape(nb, L, LANES), wuk_t, wuv, tq=128, tk=512)
                o_mla = o_mla.reshape(T, H * VH)
            else:
                ql = _head_matmul(qn, wuk_t, BF16, "q_lat")
                o_lat = _sample_attention(
                    page_table, ql.reshape(nb, L * H, KVL), qr.reshape(nb, L * H, LANES),
                    ckv.reshape(nb, L, KVL), krb.astype(F32).reshape(nb, L, LANES),
                    cache_ckv, cache_krt, P=min(32, NPG), H=H, T=L, layer=l)
                o_mla = _head_matmul(o_lat.reshape(T, H * KVL).astype(BF16), wuv, BF16, "o_mla")

            merged = _matmul([(o_gdn, wo_gdn), (o_mla, wo_mla)],
                             [(main, (CW + H * DV) // tn), (main, (CW + H * DV + D) // tn)],
                             _epi_merge, BF16, tm, tn, "merge")
            x1 = _matmul([(merged, wout)], [(x, 0)], _epi_resid, F32, tm, tn, "out_proj")

            if prompt:
                fstate = jnp.zeros((nb, FK - 1, F2), F32)
                h2 = _rmsnorm(x1, g_ffn[l], BF16, tr)
                act, new_fconv = _up_conv_act(h2, wup, ffn_conv_w[l], fstate, L=L, tm=tm, tn=512)
            else:
                fstate = state_ffn_conv[l]
                up = _matmul([(x1, wup)], [], _epi_plain, F32, tm, tn, "up_proj",
                             prenorm=g_ffn[l])
                up3 = up.reshape(nb, L, F2)
                xt = jnp.concatenate([jnp.swapaxes(fstate, 0, 1), jnp.swapaxes(up3, 0, 1)], axis=0)
                act = _ffn_conv_timemajor(xt, ffn_conv_w[l], tc=512)
                act = jnp.swapaxes(act, 0, 1).reshape(T, F)
                new_fconv = _last_rows(fstate, up3)
            x2 = _matmul([(act, wdown)], [(x1, 0)], _epi_resid, F32, tr, 1024, "down_proj")

            last_layer = l == depth - 1
            x3 = _matmul([(x2, wpg), (p_l.reshape(T, -1).astype(BF16), wpp)], [],
                         _epi_ple_final if last_layer else _epi_ple, F32, min(256, T), D, "ple",
                         prenorm=g_ple[l], rowvecs=(g_final,) if last_layer else (),
                         a_as_extra=True)
            return x3, (ckv, kr, s_new, new_gconv, new_fconv)

        xp, st = layer(xp, p_prompt[l], prompt=True)
        outs["ckv_p"].append(st[0].reshape(BP, LP // PS, PS, KVL))
        outs["kr_p"].append(st[1].reshape(BP, LP // PS, PS, ROPE))
        outs["gdn_p"].append(st[2])
        outs["gc_p"].append(st[3])
        outs["fc_p"].append(st[4])
        xs, st = layer(xs, p_sample[l], prompt=False)
        outs["ckv_s"].append(st[0].reshape(DB, TS, KVL))
        outs["kr_s"].append(st[1].reshape(DB, TS, ROPE))
        outs["gdn_s"].append(st[2])
        outs["gc_s"].append(st[3])
        outs["fc_s"].append(st[4])

    y_prompt = xp.reshape(BP, LP, D)
    y_sample = xs.reshape(DB, TS, D)
    st = lambda k: jnp.stack(outs[k])
    return (y_prompt, y_sample, st("ckv_p"), st("kr_p"), st("ckv_s"), st("kr_s"),
            st("gdn_p"), st("gdn_s"), st("gc_p"), st("gc_s"), st("fc_p"), st("fc_s"))
```

```python
import functools
import math

import jax
import jax.numpy as jnp
import numpy as np
from jax import lax
from jax.experimental import pallas as pl
from jax.experimental.pallas import tpu as pltpu

F32 = jnp.float32
BF16 = jnp.bfloat16
EPS = 1e-6
ROPE_BASE = 10000.0
HP = lax.Precision.HIGHEST
LANES = 128
SUBLANES = 8
NEG = -1e30
VMEM_LIMIT = 56 * 1024 * 1024

NT = (((1,), (1,)), ((), ()))
TN = (((0,), (0,)), ((), ()))


def _cp(sem):
    return pltpu.CompilerParams(dimension_semantics=sem, vmem_limit_bytes=VMEM_LIMIT)


def _sigmoid(x):
    return 1.0 / (1.0 + jnp.exp(-x))


def _silu(x):
    return x * _sigmoid(x)


def _bdot(a, b, dims=None):
    a = a.astype(BF16)
    b = b.astype(BF16)
    if dims is None:
        return jnp.dot(a, b, preferred_element_type=F32)
    return lax.dot_general(a, b, dims, preferred_element_type=F32)


def _rms_body(x_ref, g_ref, o_ref):
    x = x_ref[...]
    ms = jnp.mean(x * x, axis=-1, keepdims=True)
    o_ref[...] = (x * lax.rsqrt(ms + EPS) * g_ref[...]).astype(o_ref.dtype)


def _rmsnorm(x, g, out_dtype, tm):
    M, D = x.shape
    return pl.pallas_call(
        _rms_body,
        grid=(M // tm,),
        in_specs=[pl.BlockSpec((tm, D), lambda i: (i, 0)),
                  pl.BlockSpec((1, D), lambda i: (0, 0))],
        out_specs=pl.BlockSpec((tm, D), lambda i: (i, 0)),
        out_shape=jax.ShapeDtypeStruct((M, D), out_dtype),
        compiler_params=_cp(("parallel",)),
        name="rmsnorm",
    )(x, g.reshape(1, D))


def _rms_rows(x, g):
    return x * lax.rsqrt(jnp.mean(x * x, axis=-1, keepdims=True) + EPS) * g


def _mm_body(*refs, n_pairs, n_extras, epi, prenorm, a_as_extra):
    o_ref = refs[-1]
    a_raw = refs[0][...]
    a0 = _rms_rows(a_raw, refs[-2][...]).astype(BF16) if prenorm else a_raw
    accs = [jnp.dot(a0, refs[1][...], preferred_element_type=F32)]
    accs += [jnp.dot(refs[2 * p][...], refs[2 * p + 1][...], preferred_element_type=F32)
             for p in range(1, n_pairs)]
    extras = [a_raw] if a_as_extra else []
    extras += [r[...] for r in refs[2 * n_pairs:2 * n_pairs + n_extras]]
    o_ref[...] = epi(accs, extras).astype(o_ref.dtype)


def _matmul(pairs, extras, epi, out_dtype, tm, tn, name, prenorm=None, rowvecs=(),
            a_as_extra=False):
    M = pairs[0][0].shape[0]
    N = pairs[0][1].shape[1]
    in_specs, args = [], []
    for a, w in pairs:
        K = a.shape[1]
        in_specs += [pl.BlockSpec((tm, K), lambda j, i: (i, 0)),
                     pl.BlockSpec((K, tn), lambda j, i: (0, j))]
        args += [a, w]
    for e, off in extras:
        in_specs.append(pl.BlockSpec((tm, tn), lambda j, i, off=off: (i, off + j)))
        args.append(e)
    for v in rowvecs:
        in_specs.append(pl.BlockSpec((1, tn), lambda j, i: (0, j)))
        args.append(v.reshape(1, N))
    if prenorm is not None:
        K0 = pairs[0][0].shape[1]
        in_specs.append(pl.BlockSpec((1, K0), lambda j, i: (0, 0)))
        args.append(prenorm.reshape(1, K0))
    return pl.pallas_call(
        functools.partial(_mm_body, n_pairs=len(pairs), n_extras=len(extras) + len(rowvecs),
                          epi=epi, prenorm=prenorm is not None, a_as_extra=a_as_extra),
        grid=(N // tn, M // tm),
        in_specs=in_specs,
        out_specs=pl.BlockSpec((tm, tn), lambda j, i: (i, j)),
        out_shape=jax.ShapeDtypeStruct((M, N), out_dtype),
        compiler_params=_cp(("parallel", "parallel")),
        name=name,
    )(*args)


def _epi_plain(accs, extras):
    return accs[0]


def _epi_merge(accs, extras):
    return _sigmoid(extras[0]) * accs[0] + _sigmoid(extras[1]) * accs[1]


def _epi_resid(accs, extras):
    return extras[0] + accs[0]


def _epi_ple(accs, extras):
    return extras[0] + _sigmoid(accs[0]) * accs[1]


def _epi_ple_final(accs, extras):
    return _rms_rows(extras[0] + _sigmoid(accs[0]) * accs[1], extras[1])


def _headmm_body(a_ref, w_ref, o_ref):
    o_ref[...] = jnp.dot(a_ref[...], w_ref[0], preferred_element_type=F32).astype(o_ref.dtype)


def _head_matmul(a, w, out_dtype, name):
    M = a.shape[0]
    H, K, N = w.shape
    return pl.pallas_call(
        _headmm_body,
        grid=(H,),
        in_specs=[pl.BlockSpec((M, K), lambda h: (0, h)),
                  pl.BlockSpec((1, K, N), lambda h: (h, 0, 0))],
        out_specs=pl.BlockSpec((M, N), lambda h: (0, h)),
        out_shape=jax.ShapeDtypeStruct((M, H * N), out_dtype),
        compiler_params=_cp(("parallel",)),
        name=name,
    )(a, w)


def _gdn_body(cin_ref, z_ref, sm_ref, cw_ref, cs_ref, s0_ref, al_ref, dtb_ref, nw_ref,
              o_ref, sout_ref, xp, S, smp, *, NB, C, Lb, H, DK, DV, KC):
    c = pl.program_id(1)
    last = pl.num_programs(1) - 1
    HIST = KC - 1
    R0 = SUBLANES

    @pl.when(c == 0)
    def _init():
        xp[...] = jnp.zeros_like(xp)
        for n in range(NB):
            xp[n, R0 - HIST:R0, :] = cs_ref[n]
        S[...] = s0_ref[...]
        smp[...] = jnp.zeros_like(smp)

    for n in range(NB):
        xp[n, R0:R0 + Lb, :] = cin_ref[n]
        smp[n, 0:Lb, :] = sm_ref[n]

    lane = lax.broadcasted_iota(jnp.int32, (C, LANES), 1)
    row = lax.broadcasted_iota(jnp.int32, (C, LANES), 0)
    valid = row < Lb
    ri = lax.broadcasted_iota(jnp.int32, (C, C), 0)
    ci = lax.broadcasted_iota(jnp.int32, (C, C), 1)
    tril = ri >= ci
    strict = ri > ci
    eye = (ri == ci).astype(F32)
    trilf = tril.astype(F32)
    sel = (lax.broadcasted_iota(jnp.int32, (SUBLANES, LANES), 1)
           == lax.broadcasted_iota(jnp.int32, (SUBLANES, LANES), 0) + H).astype(F32)

    gates = []
    for n in range(NB):
        smv = smp[n]
        beta = jnp.where(valid & (lane < H), _sigmoid(smv), 0.0)
        xa = smv + dtb_ref[...]
        softplus = jnp.maximum(xa, 0.0) + jnp.log(1.0 + jnp.exp(-jnp.abs(xa)))
        g = jnp.where(valid & (lane >= H) & (lane < 2 * H), -jnp.exp(al_ref[...]) * softplus, 0.0)
        gc = jnp.dot(trilf, g, precision=HP, preferred_element_type=F32)
        gr = lax.dot_general(sel, gc, NT, precision=HP, preferred_element_type=F32)
        gates.append((beta, gc, gr))

    def conv(n, off):
        acc = xp[n, R0 - HIST:R0 - HIST + C, off:off + LANES] * cw_ref[0:1, off:off + LANES]
        for j in range(1, KC):
            acc = acc + (xp[n, R0 - HIST + j:R0 - HIST + j + C, off:off + LANES]
                         * cw_ref[j:j + 1, off:off + LANES])
        return _silu(acc)

    chains = [(n, h) for n in range(NB) for h in range(H)]
    st = []
    for n, h in chains:
        beta, gc, gr = gates[n]
        q = conv(n, h * DK)
        k = conv(n, H * DK + h * DK)
        v = conv(n, 2 * H * DK + h * DV)
        q = q * lax.rsqrt(jnp.sum(q * q, axis=-1, keepdims=True) + EPS) * (DK ** -0.5)
        k = k * lax.rsqrt(jnp.sum(k * k, axis=-1, keepdims=True) + EPS)
        gcol = gc[:, H + h:H + h + 1]
        grow = gr[h:h + 1, :]
        bcol = beta[:, h:h + 1]
        glast = gc[C - 1:C, H + h:H + h + 1]
        decay = jnp.where(tril, jnp.exp(jnp.where(tril, gcol - grow, 0.0)), 0.0)
        kb = k * bcol
        egc = jnp.exp(gcol)
        st.append(dict(q=q, k=k, gcol=gcol, glast=glast, decay=decay, kb=kb,
                       rhs=jnp.concatenate([v * bcol, kb * egc], axis=1), qe=q * egc))

    for s in st:
        s["pw"] = -jnp.where(strict, _bdot(s["kb"], s["k"], NT) * s["decay"], 0.0)
        s["tinv"] = eye + s["pw"]
    n_levels = int(math.ceil(math.log2(C)))
    for s in st:
        s["pw"] = _bdot(s["pw"], s["pw"])
    for _ in range(n_levels - 2):
        for s in st:
            both = _bdot(jnp.concatenate([s["pw"], s["tinv"]], axis=0), s["pw"])
            s["pw"] = both[0:C]
            s["tinv"] = s["tinv"] + both[C:2 * C]
    for s in st:
        s["tinv"] = s["tinv"] + _bdot(s["tinv"], s["pw"])

    for s in st:
        sol = _bdot(s["tinv"], s["rhs"])
        s["u"] = sol[:, 0:DV]
        s["w"] = sol[:, DV:DV + DK]
        s["qk"] = _bdot(s["q"], s["k"], NT) * s["decay"]

    for (n, h), s in zip(chains, st):
        s["S"] = S[n, h]
        ws = _bdot(jnp.concatenate([s["w"], s["qe"]], axis=0), s["S"])
        s["v_new"] = s["u"] - ws[0:C]
        s["o"] = ws[C:2 * C]

    for (n, h), s in zip(chains, st):
        o = s["o"] + _bdot(s["qk"], s["v_new"])
        S[n, h] = (s["S"] * jnp.exp(s["glast"])
                   + _bdot(s["k"] * jnp.exp(s["glast"] - s["gcol"]), s["v_new"], TN))
        on = o * lax.rsqrt(jnp.mean(o * o, axis=-1, keepdims=True) + EPS) * nw_ref[...]
        zz = z_ref[n, :, h * DV:(h + 1) * DV]
        o_ref[n, :, h * DV:(h + 1) * DV] = (on[0:Lb] * _silu(zz)).astype(o_ref.dtype)

    if Lb == C:
        for n in range(NB):
            xp[n, R0 - HIST:R0, :] = xp[n, R0 + C - HIST:R0 + C, :]

    @pl.when(c == last)
    def _fin():
        sout_ref[...] = S[...]


def _gdn(main3, small3, conv_w, conv_state, s0, a_log, dt_bias, norm_w, *, NB, C, Lb, H, DK, DV,
         out_dtype):
    B, L, _ = main3.shape
    KC, CW = conv_w.shape
    nc = L // Lb
    assert nc == 1 or Lb == C
    assert B % NB == 0
    pad = lambda vec, off: jnp.zeros((1, LANES), F32).at[0, off:off + H].set(vec.astype(F32))
    body = functools.partial(_gdn_body, NB=NB, C=C, Lb=Lb, H=H, DK=DK, DV=DV, KC=KC)
    return pl.pallas_call(
        body,
        grid=(B // NB, nc),
        in_specs=[
            pl.BlockSpec((NB, Lb, CW), lambda b, c: (b, c, 0)),
            pl.BlockSpec((NB, Lb, H * DV), lambda b, c: (b, c, CW // (H * DV))),
            pl.BlockSpec((NB, Lb, LANES), lambda b, c: (b, c, 2)),
            pl.BlockSpec((KC, CW), lambda b, c: (0, 0)),
            pl.BlockSpec((NB, KC - 1, CW), lambda b, c: (b, 0, 0)),
            pl.BlockSpec((NB, H, DK, DV), lambda b, c: (b, 0, 0, 0)),
            pl.BlockSpec((1, LANES), lambda b, c: (0, 0)),
            pl.BlockSpec((1, LANES), lambda b, c: (0, 0)),
            pl.BlockSpec((1, DV), lambda b, c: (0, 0)),
        ],
        out_specs=[
            pl.BlockSpec((NB, Lb, H * DV), lambda b, c: (b, c, 0)),
            pl.BlockSpec((NB, H, DK, DV), lambda b, c: (b, 0, 0, 0)),
        ],
        out_shape=[jax.ShapeDtypeStruct((B, L, H * DV), out_dtype),
                   jax.ShapeDtypeStruct((B, H, DK, DV), F32)],
        scratch_shapes=[pltpu.VMEM((NB, SUBLANES + C, CW), F32),
                        pltpu.VMEM((NB, H, DK, DV), F32),
                        pltpu.VMEM((NB, C, LANES), F32)],
        compiler_params=_cp(("parallel", "arbitrary")),
        name="gdn",
    )(main3, main3, small3, conv_w, conv_state, s0, pad(a_log, H), pad(dt_bias, H),
      norm_w.reshape(1, DV).astype(F32))


def _mlaproj_body(cq_ref, ckv_ref, sm_ref, cos_ref, sin_ref, wq_ref, gq_ref, gkv_ref,
                  qn_ref, qr_ref, ckv_o_ref, ckvb_ref, kr_o_ref, krb_ref, *, H, NOPE, ROPE, scale):
    cq = cq_ref[...]
    cq = (cq * lax.rsqrt(jnp.mean(cq * cq, axis=-1, keepdims=True) + EPS) * gq_ref[...]).astype(BF16)
    q = jnp.dot(cq, wq_ref[...], preferred_element_type=F32)
    HW = H * LANES
    cosp = cos_ref[...]
    sinp = sin_ref[...]
    cos_h = jnp.concatenate([cosp] * H, axis=1)
    sin_h = jnp.concatenate([sinp] * H, axis=1)
    qn_ref[...] = (q[:, 0:H * NOPE] * scale).astype(qn_ref.dtype)
    qr = q[:, H * NOPE:H * NOPE + HW] * cos_h + q[:, H * NOPE + HW:H * NOPE + 2 * HW] * sin_h
    qr_ref[...] = (qr * scale).astype(qr_ref.dtype)
    ckv = ckv_ref[...]
    ckv = ckv * lax.rsqrt(jnp.mean(ckv * ckv, axis=-1, keepdims=True) + EPS) * gkv_ref[...]
    ckv_o_ref[...] = ckv
    ckvb_ref[...] = ckv.astype(BF16)
    sm = sm_ref[...]
    kr = sm[:, 0:LANES] * cosp + sm[:, LANES:2 * LANES] * sinp
    kr_o_ref[...] = kr[:, 0:ROPE]
    krb_ref[...] = kr.astype(BF16)


def _mla_proj(main2, small2, cosp, sinp, wq, g_q, g_kv, *, tl, H, NOPE, ROPE, QL, KVL, scale, table_blocks):
    T, W = main2.shape
    cq_blk = (W - QL - KVL) // QL
    assert QL == KVL
    nt = table_blocks
    body = functools.partial(_mlaproj_body, H=H, NOPE=NOPE, ROPE=ROPE, scale=scale)
    return pl.pallas_call(
        body,
        grid=(T // tl,),
        in_specs=[
            pl.BlockSpec((tl, QL), lambda i: (i, cq_blk)),
            pl.BlockSpec((tl, KVL), lambda i: (i, cq_blk + 1)),
            pl.BlockSpec((tl, 2 * LANES), lambda i: (i, 0)),
            pl.BlockSpec((tl, LANES), lambda i: (i % nt, 0)),
            pl.BlockSpec((tl, LANES), lambda i: (i % nt, 0)),
            pl.BlockSpec(wq.shape, lambda i: (0, 0)),
            pl.BlockSpec((1, QL), lambda i: (0, 0)),
            pl.BlockSpec((1, KVL), lambda i: (0, 0)),
        ],
        out_specs=[
            pl.BlockSpec((tl, H * NOPE), lambda i: (i, 0)),
            pl.BlockSpec((tl, H * LANES), lambda i: (i, 0)),
            pl.BlockSpec((tl, KVL), lambda i: (i, 0)),
            pl.BlockSpec((tl, KVL), lambda i: (i, 0)),
            pl.BlockSpec((tl, ROPE), lambda i: (i, 0)),
            pl.BlockSpec((tl, LANES), lambda i: (i, 0)),
        ],
        out_shape=[
            jax.ShapeDtypeStruct((T, H * NOPE), BF16),
            jax.ShapeDtypeStruct((T, H * LANES), BF16),
            jax.ShapeDtypeStruct((T, KVL), F32),
            jax.ShapeDtypeStruct((T, KVL), BF16),
            jax.ShapeDtypeStruct((T, ROPE), F32),
            jax.ShapeDtypeStruct((T, LANES), BF16),
        ],
        compiler_params=_cp(("parallel",)),
        name="mla_proj",
    )(main2, main2, small2, cosp, sinp, wq, g_q.reshape(1, QL), g_kv.reshape(1, KVL))


def _pattn_body(qn_ref, qr_ref, k_ref, kr_ref, wuk_ref, wuv_ref, o_ref,
                ql_s, qr_s, m_s, l_s, acc_s, *, H, tq, tk, NOPE, VH):
    i = pl.program_id(1)
    for h in range(H):
        ql_s[h * tq:(h + 1) * tq, :] = jnp.dot(
            qn_ref[0, :, h * NOPE:(h + 1) * NOPE], wuk_ref[h], preferred_element_type=F32).astype(BF16)
        qr_s[h * tq:(h + 1) * tq, :] = qr_ref[0, :, h * LANES:(h + 1) * LANES]
    m_s[...] = jnp.full_like(m_s, NEG)
    l_s[...] = jnp.zeros_like(l_s)
    acc_s[...] = jnp.zeros_like(acc_s)
    R = H * tq
    qpos = i * tq + (lax.broadcasted_iota(jnp.int32, (R, tk), 0) & (tq - 1))
    kofs = lax.broadcasted_iota(jnp.int32, (R, tk), 1)
    nkv = (i * tq + tq + tk - 1) // tk

    def scores(j):
        start = pl.multiple_of(j * tk, tk)
        return (lax.dot_general(ql_s[...], k_ref[0, pl.ds(start, tk), :], NT,
                                preferred_element_type=F32)
                + lax.dot_general(qr_s[...], kr_ref[0, pl.ds(start, tk), :], NT,
                                  preferred_element_type=F32))

    def absorb(s, j, masked):
        start = pl.multiple_of(j * tk, tk)
        if masked:
            s = jnp.where(kofs + j * tk <= qpos, s, NEG)
        m_prev = m_s[...]
        m_new = jnp.maximum(m_prev, jnp.max(s, axis=-1, keepdims=True))
        alpha = jnp.exp(m_prev - m_new)
        p = jnp.exp(s - m_new)
        l_s[...] = alpha * l_s[...] + jnp.sum(p, axis=-1, keepdims=True)
        acc_s[...] = alpha * acc_s[...] + jnp.dot(p.astype(BF16), k_ref[0, pl.ds(start, tk), :],
                                                  preferred_element_type=F32)
        m_s[...] = m_new

    def full_chunk(j, s_cur):
        s_next = scores(j + 1)
        absorb(s_cur, j, False)
        return s_next

    s_last = lax.fori_loop(0, nkv - 1, full_chunk, scores(0))
    absorb(s_last, nkv - 1, True)
    o = (acc_s[...] / l_s[...]).astype(BF16)
    for h in range(H):
        o_ref[0, :, h * VH:(h + 1) * VH] = jnp.dot(
            o[h * tq:(h + 1) * tq, :], wuv_ref[h], preferred_element_type=F32).astype(o_ref.dtype)


def _prompt_attention(qn, qr, ckvb, krb, wuk_t, wuv, *, tq, tk):
    B, L, _ = qn.shape
    H, NOPE, KVL = wuk_t.shape
    VH = wuv.shape[2]
    assert tq & (tq - 1) == 0 and tk % tq == 0 and H % 2 == 0
    body = functools.partial(_pattn_body, H=H, tq=tq, tk=tk, NOPE=NOPE, VH=VH)
    return pl.pallas_call(
        body,
        grid=(B, L // tq),
        in_specs=[
            pl.BlockSpec((1, tq, H * NOPE), lambda b, i: (b, i, 0)),
            pl.BlockSpec((1, tq, H * LANES), lambda b, i: (b, i, 0)),
            pl.BlockSpec((1, L, KVL), lambda b, i: (b, 0, 0)),
            pl.BlockSpec((1, L, LANES), lambda b, i: (b, 0, 0)),
            pl.BlockSpec((H, NOPE, KVL), lambda b, i: (0, 0, 0)),
            pl.BlockSpec((H, KVL, VH), lambda b, i: (0, 0, 0)),
        ],
        out_specs=pl.BlockSpec((1, tq, H * VH), lambda b, i: (b, i, 0)),
        out_shape=jax.ShapeDtypeStruct((B, L, H * VH), BF16),
        scratch_shapes=[pltpu.VMEM((H * tq, KVL), BF16),
                        pltpu.VMEM((H * tq, LANES), BF16),
                        pltpu.VMEM((H * tq, 1), F32),
                        pltpu.VMEM((H * tq, 1), F32),
                        pltpu.VMEM((H * tq, KVL), F32)],
        compiler_params=_cp(("parallel", "arbitrary")),
        name="prompt_attn",
    )(qn, qr, ckvb, krb, wuk_t, wuv)


def _sattn_body(pt_ref, ql_ref, qr_ref, cn_ref, krn_ref, cc_hbm, ckt_hbm, o_ref,
                kbuf, rbuf, sem, m_s, l_s, acc_s, knew, rnew, *, P, NC, PS, ROPE, H, T, layer):
    b = pl.program_id(0)
    c = pl.program_id(1)
    step = b * NC + c
    nsteps = pl.num_programs(0) * NC
    slot = step % 2

    def page_copies(stp, sl):
        bb = stp // NC
        cc = stp % NC
        cps = []
        for p in range(P):
            pg = pt_ref[bb, cc * P + p]
            cps.append(pltpu.make_async_copy(cc_hbm.at[layer, pg], kbuf.at[sl, p], sem.at[0, sl]))
            cps.append(pltpu.make_async_copy(ckt_hbm.at[layer, pg], rbuf.at[sl, p], sem.at[1, sl]))
        return cps

    @pl.when(step == 0)
    def _first():
        for cp in page_copies(0, 0):
            cp.start()

    @pl.when(step + 1 < nsteps)
    def _prefetch():
        for cp in page_copies(step + 1, 1 - slot):
            cp.start()

    for cp in page_copies(step, slot):
        cp.wait()

    @pl.when(c == 0)
    def _init():
        m_s[...] = jnp.full_like(m_s, NEG)
        l_s[...] = jnp.zeros_like(l_s)
        acc_s[...] = jnp.zeros_like(acc_s)

    ql = ql_ref[0]
    qr = qr_ref[0]

    def update(s, vals):
        m_prev = m_s[...]
        m_new = jnp.maximum(m_prev, jnp.max(s, axis=-1, keepdims=True))
        alpha = jnp.exp(m_prev - m_new)
        p = jnp.exp(s - m_new)
        l_s[...] = alpha * l_s[...] + jnp.sum(p, axis=-1, keepdims=True)
        acc_s[...] = alpha * acc_s[...] + jnp.dot(p.astype(BF16), vals, preferred_element_type=F32)
        m_s[...] = m_new

    k = kbuf[slot].reshape(P * PS, kbuf.shape[-1]).astype(BF16)
    qr_rope = qr[:, 0:ROPE]
    s_rope = jnp.concatenate(
        [jnp.dot(qr_rope, rbuf[slot, p].astype(BF16), preferred_element_type=F32) for p in range(P)],
        axis=1)
    s = lax.dot_general(ql, k, NT, preferred_element_type=F32) + s_rope
    update(s, k)

    @pl.when(c == NC - 1)
    def _fin():
        knew[...] = jnp.zeros_like(knew)
        rnew[...] = jnp.zeros_like(rnew)
        knew[0:T, :] = cn_ref[0]
        rnew[0:T, :] = krn_ref[0]
        kn = knew[...].astype(BF16)
        rn = rnew[...].astype(BF16)
        s2 = (lax.dot_general(ql, kn, NT, preferred_element_type=F32)
              + lax.dot_general(qr, rn, NT, preferred_element_type=F32))
        R = T * H
        trow = lax.broadcasted_iota(jnp.int32, (R, LANES), 0) // H
        kcol = lax.broadcasted_iota(jnp.int32, (R, LANES), 1)
        s2 = jnp.where(kcol <= trow, s2, NEG)
        update(s2, kn)
        o_ref[0] = acc_s[...] / l_s[...]


def _sample_attention(page_table, ql, qr, cn, krn, cache_ckv, cache_krt, *, P, H, T, layer):
    DB, NPG = page_table.shape
    _, _, PS, KVL = cache_ckv.shape
    ROPE = cache_krt.shape[2]
    NC = NPG // P
    R = T * H
    body = functools.partial(_sattn_body, P=P, NC=NC, PS=PS, ROPE=ROPE, H=H, T=T, layer=layer)
    grid_spec = pltpu.PrefetchScalarGridSpec(
        num_scalar_prefetch=1,
        grid=(DB, NC),
        in_specs=[
            pl.BlockSpec((1, R, KVL), lambda b, c, pt: (b, 0, 0)),
            pl.BlockSpec((1, R, LANES), lambda b, c, pt: (b, 0, 0)),
            pl.BlockSpec((1, T, KVL), lambda b, c, pt: (b, 0, 0)),
            pl.BlockSpec((1, T, LANES), lambda b, c, pt: (b, 0, 0)),
            pl.BlockSpec(memory_space=pl.ANY),
            pl.BlockSpec(memory_space=pl.ANY),
        ],
        out_specs=pl.BlockSpec((1, R, KVL), lambda b, c, pt: (b, 0, 0)),
        scratch_shapes=[pltpu.VMEM((2, P, PS, KVL), F32),
                        pltpu.VMEM((2, P, ROPE, PS), F32),
                        pltpu.SemaphoreType.DMA((2, 2)),
                        pltpu.VMEM((R, 1), F32),
                        pltpu.VMEM((R, 1), F32),
                        pltpu.VMEM((R, KVL), F32),
                        pltpu.VMEM((LANES, KVL), F32),
                        pltpu.VMEM((LANES, LANES), F32)],
    )
    return pl.pallas_call(
        body,
        grid_spec=grid_spec,
        out_shape=jax.ShapeDtypeStruct((DB, R, KVL), F32),
        compiler_params=_cp(("arbitrary", "arbitrary")),
        name="sample_attn",
    )(page_table, ql, qr, cn, krn, cache_ckv, cache_krt)


class _PageStream:
    def __init__(self, page_table, ql, qr, cn, krn, cache_ckv, cache_krt, *, P, H, T, layer):
        self.nseq, npg = page_table.shape
        _, _, self.PS, self.KVL = cache_ckv.shape
        self.ROPE = cache_krt.shape[2]
        self.P, self.H, self.T, self.layer = P, H, T, layer
        self.NC = npg // P
        assert self.NC % 2 == 0
        self.R = T * H
        self.page_table = page_table
        self.args = (ql, qr, cn, krn, cache_ckv, cache_krt)

    def in_specs(self, step_of):
        seq = lambda *g: (jnp.minimum(step_of(*g[:-1]), self.nseq - 1), 0, 0)
        return [pl.BlockSpec((1, self.R, self.KVL), seq),
                pl.BlockSpec((1, self.R, LANES), seq),
                pl.BlockSpec((1, self.T, self.KVL), seq),
                pl.BlockSpec((1, self.T, LANES), seq),
                pl.BlockSpec(memory_space=pl.ANY),
                pl.BlockSpec(memory_space=pl.ANY)], pl.BlockSpec((1, self.R, self.KVL), seq)

    def out_shape(self):
        return jax.ShapeDtypeStruct((self.nseq, self.R, self.KVL), F32)

    def scratch_shapes(self):
        return [pltpu.VMEM((2, self.P, self.PS, self.KVL), F32),
                pltpu.VMEM((2, self.P, self.ROPE, self.PS), F32),
                pltpu.SemaphoreType.DMA((2, 2)),
                pltpu.VMEM((self.R, 1), F32),
                pltpu.VMEM((self.R, 1), F32),
                pltpu.VMEM((self.R, self.KVL), F32),
                pltpu.VMEM((LANES, self.KVL), F32),
                pltpu.VMEM((LANES, LANES), F32)]

    N_IN = 6
    N_SCRATCH = 8

    def bind(self, step, total_steps, pt_ref, in_refs, o_ref, scratch):
        ql_ref, qr_ref, cn_ref, krn_ref, cc_hbm, ckt_hbm = in_refs
        kbuf, rbuf, sem, m_s, l_s, acc_s, knew, rnew = scratch
        P, NC, PS, ROPE, H, T, layer, nseq = (self.P, self.NC, self.PS, self.ROPE, self.H,
                                              self.T, self.layer, self.nseq)

        def when_active(fn):
            if nseq >= total_steps:
                fn()
            else:
                pl.when(step < nseq)(fn)

        def copies(seq, c, sl):
            cps = []
            for p in range(P):
                pg = pt_ref[seq, c * P + p]
                cps.append(pltpu.make_async_copy(cc_hbm.at[layer, pg], kbuf.at[sl, p], sem.at[0, sl]))
                cps.append(pltpu.make_async_copy(ckt_hbm.at[layer, pg], rbuf.at[sl, p], sem.at[1, sl]))
            return cps

        def start(seq, c, sl):
            for cp in copies(seq, c, sl):
                cp.start()

        def update(s, vals):
            m_prev = m_s[...]
            m_new = jnp.maximum(m_prev, jnp.max(s, axis=-1, keepdims=True))
            alpha = jnp.exp(m_prev - m_new)
            p = jnp.exp(s - m_new)
            l_s[...] = alpha * l_s[...] + jnp.sum(p, axis=-1, keepdims=True)
            acc_s[...] = alpha * acc_s[...] + jnp.dot(p.astype(BF16), vals, preferred_element_type=F32)
            m_s[...] = m_new

        def pre(u):
            def go():
                if u == 0:
                    pl.when(step == 0)(lambda: start(0, 0, 0))
                    m_s[...] = jnp.full_like(m_s, NEG)
                    l_s[...] = jnp.zeros_like(l_s)
                    acc_s[...] = jnp.zeros_like(acc_s)
                if u + 1 < NC:
                    start(step, u + 1, (u + 1) % 2)
                else:
                    pl.when(step + 1 < nseq)(lambda: start(step + 1, 0, 0))
            when_active(go)

        def post(u):
            def go():
                sl = u % 2
                for cp in copies(step, u, sl):
                    cp.wait()
                ql = ql_ref[0]
                qr = qr_ref[0]
                k = kbuf[sl].reshape(P * PS, self.KVL).astype(BF16)
                qr_rope = qr[:, 0:ROPE]
                s_rope = jnp.concatenate(
                    [jnp.dot(qr_rope, rbuf[sl, p].astype(BF16), preferred_element_type=F32)
                     for p in range(P)], axis=1)
                update(lax.dot_general(ql, k, NT, preferred_element_type=F32) + s_rope, k)
                if u == NC - 1:
                    knew[...] = jnp.zeros_like(knew)
                    rnew[...] = jnp.zeros_like(rnew)
                    knew[0:T, :] = cn_ref[0]
                    rnew[0:T, :] = krn_ref[0]
                    kn = knew[...].astype(BF16)
                    rn = rnew[...].astype(BF16)
                    s2 = (lax.dot_general(ql, kn, NT, preferred_element_type=F32)
                          + lax.dot_general(qr, rn, NT, preferred_element_type=F32))
                    trow = lax.broadcasted_iota(jnp.int32, (T * H, LANES), 0) // H
                    kcol = lax.broadcasted_iota(jnp.int32, (T * H, LANES), 1)
                    update(jnp.where(kcol <= trow, s2, NEG), kn)
                    o_ref[0] = acc_s[...] / l_s[...]
            when_active(go)

        return pre, post


def _inproj_stream_body(*refs, stream, n_steps_i, tm):
    pt_ref, a_ref, w_ref = refs[0:3]
    s_in = refs[3:3 + stream.N_IN]
    o_ref, so_ref = refs[3 + stream.N_IN:5 + stream.N_IN]
    s_scr = refs[5 + stream.N_IN:]
    step = pl.program_id(0) * n_steps_i + pl.program_id(1)
    total = pl.num_programs(0) * n_steps_i
    pre, post = stream.bind(step, total, pt_ref, s_in, so_ref, s_scr)
    rows = tm // stream.NC
    for u in range(stream.NC):
        pre(u)
        o_ref[u * rows:(u + 1) * rows, :] = jnp.dot(
            a_ref[u * rows:(u + 1) * rows, :], w_ref[...], preferred_element_type=F32
        ).astype(o_ref.dtype)
        post(u)


def _matmul_with_stream(a, w, out_dtype, tm, tn, name, stream):
    M, K = a.shape
    N = w.shape[1]
    nj, ni = N // tn, M // tm
    assert stream.nseq <= nj * ni and tm % (stream.NC * SUBLANES * 2) == 0
    step_of = lambda j, i: j * ni + i
    s_in_specs, s_out_spec = stream.in_specs(step_of)
    grid_spec = pltpu.PrefetchScalarGridSpec(
        num_scalar_prefetch=1,
        grid=(nj, ni),
        in_specs=[pl.BlockSpec((tm, K), lambda j, i, pt: (i, 0)),
                  pl.BlockSpec((K, tn), lambda j, i, pt: (0, j))] + s_in_specs,
        out_specs=[pl.BlockSpec((tm, tn), lambda j, i, pt: (i, j)), s_out_spec],
        scratch_shapes=stream.scratch_shapes(),
    )
    return pl.pallas_call(
        functools.partial(_inproj_stream_body, stream=stream, n_steps_i=ni, tm=tm),
        grid_spec=grid_spec,
        out_shape=[jax.ShapeDtypeStruct((M, N), out_dtype), stream.out_shape()],
        compiler_params=_cp(("arbitrary", "arbitrary")),
        name=name,
    )(stream.page_table, a, w, *stream.args)


def _upconv_body(*refs, tm, KC, tiles_per_seq, stream, n_steps_i):
    if stream is None:
        (a_ref, wg_ref, wv_ref, cwg_ref, cwv_ref, sg_ref, sv_ref,
         act_ref, ng_ref, nv_ref, xg, xv) = refs
        n_slices = 1
        pre = post = lambda u: None
    else:
        pt_ref = refs[0]
        a_ref, wg_ref, wv_ref, cwg_ref, cwv_ref, sg_ref, sv_ref = refs[1:8]
        s_in = refs[8:8 + stream.N_IN]
        act_ref, ng_ref, nv_ref, so_ref = refs[8 + stream.N_IN:12 + stream.N_IN]
        xg, xv = refs[12 + stream.N_IN:14 + stream.N_IN]
        s_scr = refs[14 + stream.N_IN:]
        n_slices = stream.NC
        step = pl.program_id(0) * n_steps_i + pl.program_id(1)
        total = pl.num_programs(0) * n_steps_i
        pre, post = stream.bind(step, total, pt_ref, s_in, so_ref, s_scr)
    i = pl.program_id(1)
    HIST = KC - 1
    R0 = SUBLANES

    @pl.when(i % tiles_per_seq == 0)
    def _init():
        xg[R0 - HIST:R0, :] = sg_ref[0]
        xv[R0 - HIST:R0, :] = sv_ref[0]

    rows = tm // n_slices
    for u in range(n_slices):
        pre(u)
        a = a_ref[u * rows:(u + 1) * rows, :]
        xg[R0 + u * rows:R0 + (u + 1) * rows, :] = jnp.dot(a, wg_ref[...], preferred_element_type=F32)
        xv[R0 + u * rows:R0 + (u + 1) * rows, :] = jnp.dot(a, wv_ref[...], preferred_element_type=F32)
        post(u)

    def conv(x, w_ref):
        acc = x[R0 - HIST:R0 - HIST + tm, :] * w_ref[0:1, :]
        for j in range(1, KC):
            acc = acc + x[R0 - HIST + j:R0 - HIST + j + tm, :] * w_ref[j:j + 1, :]
        return acc

    act_ref[...] = (_silu(conv(xg, cwg_ref)) * conv(xv, cwv_ref)).astype(act_ref.dtype)
    tail_g = xg[R0 + tm - HIST:R0 + tm, :]
    tail_v = xv[R0 + tm - HIST:R0 + tm, :]
    ng_ref[0] = tail_g
    nv_ref[0] = tail_v
    xg[R0 - HIST:R0, :] = tail_g
    xv[R0 - HIST:R0, :] = tail_v


def _up_conv_act(h, w_up, conv_w, state, *, L, tm, tn, stream=None):
    T, D = h.shape
    F2 = w_up.shape[1]
    F = F2 // 2
    B = T // L
    KC = conv_w.shape[0]
    nv = F // tn
    tps = L // tm
    nj, ni = F // tn, T // tm
    body = functools.partial(_upconv_body, tm=tm, KC=KC, tiles_per_seq=tps, stream=stream,
                             n_steps_i=ni)
    in_specs = [
        pl.BlockSpec((tm, D), lambda j, i, *_: (i, 0)),
        pl.BlockSpec((D, tn), lambda j, i, *_: (0, j)),
        pl.BlockSpec((D, tn), lambda j, i, *_: (0, j + nv)),
        pl.BlockSpec((KC, tn), lambda j, i, *_: (0, j)),
        pl.BlockSpec((KC, tn), lambda j, i, *_: (0, j + nv)),
        pl.BlockSpec((1, KC - 1, tn), lambda j, i, *_: (i // tps, 0, j)),
        pl.BlockSpec((1, KC - 1, tn), lambda j, i, *_: (i // tps, 0, j + nv)),
    ]
    out_specs = [
        pl.BlockSpec((tm, tn), lambda j, i, *_: (i, j)),
        pl.BlockSpec((1, KC - 1, tn), lambda j, i, *_: (i // tps, 0, j)),
        pl.BlockSpec((1, KC - 1, tn), lambda j, i, *_: (i // tps, 0, j)),
    ]
    out_shape = [jax.ShapeDtypeStruct((T, F), BF16),
                 jax.ShapeDtypeStruct((B, KC - 1, F), F32),
                 jax.ShapeDtypeStruct((B, KC - 1, F), F32)]
    scratch = [pltpu.VMEM((SUBLANES + tm, tn), F32), pltpu.VMEM((SUBLANES + tm, tn), F32)]
    args = (h, w_up, w_up, conv_w, conv_w, state, state)
    if stream is None:
        act, ng, nvv = pl.pallas_call(
            body, grid=(nj, ni), in_specs=in_specs, out_specs=out_specs, out_shape=out_shape,
            scratch_shapes=scratch, compiler_params=_cp(("parallel", "arbitrary")),
            name="up_conv_act",
        )(*args)
        return act, jnp.concatenate([ng, nvv], axis=2), None
    assert stream.nseq <= nj * ni and tm % (stream.NC * SUBLANES * 2) == 0
    s_in_specs, s_out_spec = stream.in_specs(lambda j, i: j * ni + i)
    grid_spec = pltpu.PrefetchScalarGridSpec(
        num_scalar_prefetch=1, grid=(nj, ni), in_specs=in_specs + s_in_specs,
        out_specs=out_specs + [s_out_spec], scratch_shapes=scratch + stream.scratch_shapes())
    act, ng, nvv, s_out = pl.pallas_call(
        body, grid_spec=grid_spec, out_shape=out_shape + [stream.out_shape()],
        compiler_params=_cp(("arbitrary", "arbitrary")), name="up_conv_act",
    )(stream.page_table, *args, *stream.args)
    return act, jnp.concatenate([ng, nvv], axis=2), s_out


def _ffnconv_tm_body(xg_ref, xv_ref, wg_ref, wv_ref, o_ref, *, T, KC):
    for t in range(T):
        g = xg_ref[t] * wg_ref[0:1, :]
        v = xv_ref[t] * wv_ref[0:1, :]
        for j in range(1, KC):
            g = g + xg_ref[t + j] * wg_ref[j:j + 1, :]
            v = v + xv_ref[t + j] * wv_ref[j:j + 1, :]
        o_ref[t] = (_silu(g) * v).astype(o_ref.dtype)


def _ffn_conv_timemajor(xp, conv_w, *, tc):
    TT, NB, F2 = xp.shape
    F = F2 // 2
    KC = conv_w.shape[0]
    T = TT - (KC - 1)
    nv = F // tc
    body = functools.partial(_ffnconv_tm_body, T=T, KC=KC)
    return pl.pallas_call(
        body,
        grid=(F // tc,),
        in_specs=[
            pl.BlockSpec((TT, NB, tc), lambda j: (0, 0, j)),
            pl.BlockSpec((TT, NB, tc), lambda j: (0, 0, j + nv)),
            pl.BlockSpec((KC, tc), lambda j: (0, j)),
            pl.BlockSpec((KC, tc), lambda j: (0, j + nv)),
        ],
        out_specs=pl.BlockSpec((T, NB, tc), lambda j: (0, 0, j)),
        out_shape=jax.ShapeDtypeStruct((T, NB, F), BF16),
        compiler_params=_cp(("parallel",)),
        name="ffn_conv_tm",
    )(xp, xp, conv_w, conv_w)


def _prep_weights(w_in, w_uq, w_uk, w_uv, *, H, DK, DV, QL, KVL, NOPE, ROPE, VH, D):
    conv_ch = 2 * H * DK + H * DV
    sizes = (conv_ch, H * DV, H, H, QL, KVL, ROPE, D, D)
    offs = np.cumsum((0,) + sizes)
    seg = lambda n: w_in[:, offs[n]:offs[n + 1]]
    w_conv, w_z, w_b, w_a, w_cq, w_ckv, w_kr, w_ga, w_gb = (seg(n) for n in range(9))
    half = ROPE // 2
    w_main = jnp.concatenate([w_conv, w_z, w_ga, w_gb, w_cq, w_ckv], axis=1).astype(BF16)
    zc = lambda n: jnp.zeros((D, n), w_in.dtype)
    w_kr_sw = jnp.concatenate([w_kr[:, half:], w_kr[:, :half]], axis=1)
    w_small = jnp.concatenate([w_kr, zc(LANES - ROPE), w_kr_sw, zc(LANES - ROPE),
                               w_b, w_a, zc(LANES - 2 * H)], axis=1).astype(BF16)
    wq = w_uq.reshape(QL, H, NOPE + ROPE)
    wq_n = wq[:, :, :NOPE].reshape(QL, H * NOPE)
    wq_r = wq[:, :, NOPE:]
    zr = jnp.zeros((QL, H, LANES - ROPE), w_uq.dtype)
    wq_rp = jnp.concatenate([wq_r, zr], axis=2).reshape(QL, H * LANES)
    wq_rs = jnp.concatenate([wq_r[:, :, half:], wq_r[:, :, :half], zr], axis=2).reshape(QL, H * LANES)
    wq_all = jnp.concatenate([wq_n, wq_rp, wq_rs], axis=1).astype(BF16)
    wuk_t = jnp.transpose(w_uk, (1, 2, 0)).astype(BF16)
    wuv = jnp.transpose(w_uv, (1, 0, 2)).astype(BF16)
    return w_main, w_small, wq_all, wuk_t, wuv


def _rope_tables(pos, rope):
    half = rope // 2
    inv = np.exp(-math.log(ROPE_BASE) * np.arange(half, dtype=np.float64) / half)
    ang = np.asarray(pos).astype(np.float64)[:, None] * inv[None, :]
    c, s = np.cos(ang).astype(np.float32), np.sin(ang).astype(np.float32)
    z = np.zeros((ang.shape[0], LANES - rope), np.float32)
    return np.concatenate([c, c, z], axis=1), np.concatenate([-s, s, z], axis=1)


def _last_rows(hist, x):
    n, L = hist.shape[1], x.shape[1]
    if L >= n:
        return x[:, L - n:]
    return jnp.concatenate([hist[:, L:], x], axis=1)


def kernel(x_prompt, x_sample, cache_ckv, cache_krope, page_table, state_gdn, state_gdn_conv,
           state_ffn_conv, p_prompt, p_sample, g_mix, w_in, gdn_conv_w, gdn_A_log, gdn_dt_bias,
           gdn_norm_w, w_o_gdn, g_q_a, w_uq, g_kv_a, w_uk, w_uv, w_o_mla, w_out, g_ffn, w_up,
           ffn_conv_w, w_down, g_ple, w_ple_gate, w_ple_proj, g_final):
    BP, LP, D = x_prompt.shape
    DB, TS, _ = x_sample.shape
    depth = w_in.shape[0]
    H, DK, DV = state_gdn.shape[2], state_gdn.shape[3], state_gdn.shape[4]
    KC = gdn_conv_w.shape[1]
    CW = gdn_conv_w.shape[2]
    QL = g_q_a.shape[1]
    KVL, MH, NOPE = w_uk.shape[1], w_uk.shape[2], w_uk.shape[3]
    VH = w_uv.shape[3]
    ROPE = cache_krope.shape[3]
    PS = cache_ckv.shape[2]
    NPG = page_table.shape[1]
    F2 = w_up.shape[2]
    F = F2 // 2
    FK = ffn_conv_w.shape[1]
    assert H == MH
    scale = (NOPE + ROPE) ** -0.5
    past_len = NPG * PS
    W_MAIN = CW + H * DV + 2 * D + QL + KVL

    cos_p, sin_p = _rope_tables(np.arange(LP), ROPE)
    cos_s, sin_s = _rope_tables(past_len + np.arange(TS), ROPE)
    cos_s = np.tile(cos_s, (DB, 1))
    sin_s = np.tile(sin_s, (DB, 1))
    cache_krt = jnp.swapaxes(cache_krope, 2, 3)

    xp = x_prompt.reshape(BP * LP, D)
    xs = x_sample.reshape(DB * TS, D)
    outs = {k: [] for k in ("ckv_p", "kr_p", "ckv_s", "kr_s", "gdn_p", "gdn_s",
                            "gc_p", "gc_s", "fc_p", "fc_s")}

    for l in range(depth):
        w_main, w_small, wq_all, wuk_t, wuv = _prep_weights(
            w_in[l], w_uq[l], w_uk[l], w_uv[l], H=H, DK=DK, DV=DV, QL=QL, KVL=KVL,
            NOPE=NOPE, ROPE=ROPE, VH=VH, D=D)
        wo_gdn = w_o_gdn[l].astype(BF16)
        wo_mla = w_o_mla[l].astype(BF16)
        wout = w_out[l].astype(BF16)
        wup = w_up[l].astype(BF16)
        wdown = w_down[l].astype(BF16)
        wpg = w_ple_gate[l].astype(BF16)
        wpp = w_ple_proj[l].astype(BF16)

        TN = 1024
        TN_UP = 512

        def tiles(T):
            return min(512, T), min(1024, T)

        def front(x, *, prompt, stream):
            T = x.shape[0]
            tr, tm = tiles(T)
            tn = TN
            h = _rmsnorm(x, g_mix[l], BF16, tr)
            if stream is None:
                main = _matmul([(h, w_main)], [], _epi_plain, F32, tm, tn, "in_proj")
                stream_out = None
            else:
                main, stream_out = _matmul_with_stream(h, w_main, F32, tm, tn, "in_proj", stream)
            small = _matmul([(h, w_small)], [], _epi_plain, F32, tr, 3 * LANES, "in_proj_small")
            if prompt:
                nb, L = BP, LP
                conv_state = jnp.zeros((nb, KC - 1, CW), F32)
                s0 = jnp.zeros((nb, H, DK, DV), F32)
                cos_t, sin_t, chunk = cos_p, sin_p, 64
            else:
                nb, L = DB, TS
                conv_state = state_gdn_conv[l]
                s0 = state_gdn[l]
                cos_t, sin_t, chunk = cos_s, sin_s, SUBLANES
            tl = min(256, T)
            tblk = cos_t.shape[0] // tl
            main3 = main.reshape(nb, L, W_MAIN)
            small3 = small.reshape(nb, L, 3 * LANES)
            o_gdn, s_new = _gdn(main3, small3, gdn_conv_w[l], conv_state, s0, gdn_A_log[l],
                                gdn_dt_bias[l], gdn_norm_w[l], NB=2, C=chunk, Lb=min(chunk, L),
                                H=H, DK=DK, DV=DV, out_dtype=BF16 if prompt else F32)
            o_gdn = o_gdn.reshape(T, H * DV).astype(BF16)
            new_gconv = _last_rows(conv_state, main3[:, :, :CW])

            qn, qr, ckv, ckvb, kr, krb = _mla_proj(
                main, small, jnp.asarray(cos_t), jnp.asarray(sin_t), wq_all, g_q_a[l], g_kv_a[l],
                tl=tl, H=H, NOPE=NOPE, ROPE=ROPE, QL=QL, KVL=KVL, scale=scale, table_blocks=tblk)
            return dict(x=x, main=main, o_gdn=o_gdn, s_new=s_new, new_gconv=new_gconv, qn=qn, qr=qr,
                        ckv=ckv, ckvb=ckvb, kr=kr, krb=krb, stream_out=stream_out, nb=nb, L=L)

        def back(c, o_mla, p_l, *, prompt, stream):
            x, main, o_gdn, nb, L = c["x"], c["main"], c["o_gdn"], c["nb"], c["L"]
            T = x.shape[0]
            tr, tm = tiles(T)
            tn = TN
            stream_out = None
            merged = _matmul([(o_gdn, wo_gdn), (o_mla, wo_mla)],
                             [(main, (CW + H * DV) // tn), (main, (CW + H * DV + D) // tn)],
                             _epi_merge, BF16, tm, tn, "merge")
            x1 = _matmul([(merged, wout)], [(x, 0)], _epi_resid, F32, tm, tn, "out_proj")

            if prompt:
                fstate = jnp.zeros((nb, FK - 1, F2), F32)
                h2 = _rmsnorm(x1, g_ffn[l], BF16, tr)
                act, new_fconv, stream_out = _up_conv_act(h2, wup, ffn_conv_w[l], fstate, L=L, tm=tm,
                                                          tn=TN_UP, stream=stream)
            else:
                fstate = state_ffn_conv[l]
                up = _matmul([(x1, wup)], [], _epi_plain, F32, tm, tn, "up_proj",
                             prenorm=g_ffn[l])
                up3 = up.reshape(nb, L, F2)
                xt = jnp.concatenate([jnp.swapaxes(fstate, 0, 1), jnp.swapaxes(up3, 0, 1)], axis=0)
                act = _ffn_conv_timemajor(xt, ffn_conv_w[l], tc=512)
                act = jnp.swapaxes(act, 0, 1).reshape(T, F)
                new_fconv = _last_rows(fstate, up3)
            x2 = _matmul([(act, wdown)], [(x1, 0)], _epi_resid, F32, tr, 1024, "down_proj")

            last_layer = l == depth - 1
            x3 = _matmul([(x2, wpg), (p_l.reshape(T, -1).astype(BF16), wpp)], [],
                         _epi_ple_final if last_layer else _epi_ple, F32, min(256, T), D, "ple",
                         prenorm=g_ple[l], rowvecs=(g_final,) if last_layer else (),
                         a_as_extra=True)
            return x3, new_fconv, stream_out

        cs = front(xs, prompt=False, stream=None)
        ql3 = _head_matmul(cs["qn"], wuk_t, BF16, "q_lat").reshape(DB, TS * H, KVL)
        qr3 = cs["qr"].reshape(DB, TS * H, LANES)
        cn3 = cs["ckv"].reshape(DB, TS, KVL)
        krn3 = cs["krb"].astype(F32).reshape(DB, TS, LANES)

        TP = BP * LP
        tm_p = tiles(TP)[1]
        n_a = min(DB, (W_MAIN // TN) * (TP // tm_p))
        n_b = min(DB - n_a, (F // TN_UP) * (TP // tm_p))
        p_stream = NPG // 8 if NPG % 16 == 0 else 0

        def page_stream(lo, hi):
            if hi == lo or p_stream == 0:
                return None
            return _PageStream(page_table[lo:hi], ql3[lo:hi], qr3[lo:hi], cn3[lo:hi], krn3[lo:hi],
                               cache_ckv, cache_krt, P=p_stream, H=H, T=TS, layer=l)

        if p_stream == 0:
            n_a = n_b = 0

        cp = front(xp, prompt=True, stream=page_stream(0, n_a))
        o_mla_p = _prompt_attention(
            cp["qn"].reshape(BP, LP, H * NOPE), cp["qr"].reshape(BP, LP, H * LANES),
            cp["ckvb"].reshape(BP, LP, KVL), cp["krb"].reshape(BP, LP, LANES), wuk_t, wuv,
            tq=128, tk=512).reshape(TP, H * VH)
        xp, fconv_p, so_b = back(cp, o_mla_p, p_prompt[l], prompt=True,
                                 stream=page_stream(n_a, n_a + n_b))
        outs["ckv_p"].append(cp["ckv"].reshape(BP, LP // PS, PS, KVL))
        outs["kr_p"].append(cp["kr"].reshape(BP, LP // PS, PS, ROPE))
        outs["gdn_p"].append(cp["s_new"])
        outs["gc_p"].append(cp["new_gconv"])
        outs["fc_p"].append(fconv_p)

        parts = [o for o in (cp["stream_out"], so_b) if o is not None]
        n_c = n_a + n_b
        if n_c < DB:
            parts.append(_sample_attention(
                page_table[n_c:], ql3[n_c:], qr3[n_c:], cn3[n_c:], krn3[n_c:],
                cache_ckv, cache_krt, P=min(32, NPG), H=H, T=TS, layer=l))
        o_lat = parts[0] if len(parts) == 1 else jnp.concatenate(parts, axis=0)
        o_mla_s = _head_matmul(o_lat.reshape(DB * TS, H * KVL).astype(BF16), wuv, BF16, "o_mla")
        xs, fconv_s, _ = back(cs, o_mla_s, p_sample[l], prompt=False, stream=None)
        outs["ckv_s"].append(cs["ckv"].reshape(DB, TS, KVL))
        outs["kr_s"].append(cs["kr"].reshape(DB, TS, ROPE))
        outs["gdn_s"].append(cs["s_new"])
        outs["gc_s"].append(cs["new_gconv"])
        outs["fc_s"].append(fconv_s)

    y_prompt = xp.reshape(BP, LP, D)
    y_sample = xs.reshape(DB, TS, D)
    st = lambda k: jnp.stack(outs[k])
    return (y_prompt, y_sample, st("ckv_p"), st("kr_p"), st("ckv_s"), st("kr_s"),
            st("gdn_p"), st("gdn_s"), st("gc_p"), st("gc_s"), st("fc_p"), st("fc_s"))
```

```python
import functools
import math

import jax
import jax.numpy as jnp
import numpy as np
from jax import lax
from jax.experimental import pallas as pl
from jax.experimental.pallas import tpu as pltpu

F32 = jnp.float32
BF16 = jnp.bfloat16
EPS = 1e-6
ROPE_BASE = 10000.0
HP = lax.Precision.HIGHEST
LANES = 128
SUBLANES = 8
NEG = -1e30
VMEM_LIMIT = 56 * 1024 * 1024

NT = (((1,), (1,)), ((), ()))
TN = (((0,), (0,)), ((), ()))


def _cp(sem):
    return pltpu.CompilerParams(dimension_semantics=sem, vmem_limit_bytes=VMEM_LIMIT)


def _sigmoid(x):
    return 1.0 / (1.0 + jnp.exp(-x))


def _silu(x):
    return x * _sigmoid(x)


def _bdot(a, b, dims=None):
    a = a.astype(BF16)
    b = b.astype(BF16)
    if dims is None:
        return jnp.dot(a, b, preferred_element_type=F32)
    return lax.dot_general(a, b, dims, preferred_element_type=F32)


def _rms_body(x_ref, g_ref, o_ref):
    x = x_ref[...]
    ms = jnp.mean(x * x, axis=-1, keepdims=True)
    o_ref[...] = (x * lax.rsqrt(ms + EPS) * g_ref[...]).astype(o_ref.dtype)


def _rmsnorm(x, g, out_dtype, tm):
    M, D = x.shape
    return pl.pallas_call(
        _rms_body,
        grid=(M // tm,),
        in_specs=[pl.BlockSpec((tm, D), lambda i: (i, 0)),
                  pl.BlockSpec((1, D), lambda i: (0, 0))],
        out_specs=pl.BlockSpec((tm, D), lambda i: (i, 0)),
        out_shape=jax.ShapeDtypeStruct((M, D), out_dtype),
        compiler_params=_cp(("parallel",)),
        name="rmsnorm",
    )(x, g.reshape(1, D))


def _rms_rows(x, g):
    return x * lax.rsqrt(jnp.mean(x * x, axis=-1, keepdims=True) + EPS) * g


def _mm_body(*refs, n_pairs, n_extras, epi, prenorm, a_as_extra):
    o_ref = refs[-1]
    a_raw = refs[0][...]
    a0 = _rms_rows(a_raw, refs[-2][...]).astype(BF16) if prenorm else a_raw
    accs = [jnp.dot(a0, refs[1][...], preferred_element_type=F32)]
    accs += [jnp.dot(refs[2 * p][...], refs[2 * p + 1][...], preferred_element_type=F32)
             for p in range(1, n_pairs)]
    extras = [a_raw] if a_as_extra else []
    extras += [r[...] for r in refs[2 * n_pairs:2 * n_pairs + n_extras]]
    o_ref[...] = epi(accs, extras).astype(o_ref.dtype)


def _matmul(pairs, extras, epi, out_dtype, tm, tn, name, prenorm=None, rowvecs=(),
            a_as_extra=False):
    M = pairs[0][0].shape[0]
    N = pairs[0][1].shape[1]
    in_specs, args = [], []
    for a, w in pairs:
        K = a.shape[1]
        in_specs += [pl.BlockSpec((tm, K), lambda j, i: (i, 0)),
                     pl.BlockSpec((K, tn), lambda j, i: (0, j))]
        args += [a, w]
    for e, off in extras:
        in_specs.append(pl.BlockSpec((tm, tn), lambda j, i, off=off: (i, off + j)))
        args.append(e)
    for v in rowvecs:
        in_specs.append(pl.BlockSpec((1, tn), lambda j, i: (0, j)))
        args.append(v.reshape(1, N))
    if prenorm is not None:
        K0 = pairs[0][0].shape[1]
        in_specs.append(pl.BlockSpec((1, K0), lambda j, i: (0, 0)))
        args.append(prenorm.reshape(1, K0))
    return pl.pallas_call(
        functools.partial(_mm_body, n_pairs=len(pairs), n_extras=len(extras) + len(rowvecs),
                          epi=epi, prenorm=prenorm is not None, a_as_extra=a_as_extra),
        grid=(N // tn, M // tm),
        in_specs=in_specs,
        out_specs=pl.BlockSpec((tm, tn), lambda j, i: (i, j)),
        out_shape=jax.ShapeDtypeStruct((M, N), out_dtype),
        compiler_params=_cp(("parallel", "parallel")),
        name=name,
    )(*args)


def _epi_plain(accs, extras):
    return accs[0]


def _epi_merge(accs, extras):
    return _sigmoid(extras[0]) * accs[0] + _sigmoid(extras[1]) * accs[1]


def _epi_resid(accs, extras):
    return extras[0] + accs[0]


def _epi_ple(accs, extras):
    return extras[0] + _sigmoid(accs[0]) * accs[1]


def _epi_ple_final(accs, extras):
    return _rms_rows(extras[0] + _sigmoid(accs[0]) * accs[1], extras[1])


def _headmm_body(a_ref, w_ref, o_ref):
    o_ref[...] = jnp.dot(a_ref[...], w_ref[0], preferred_element_type=F32).astype(o_ref.dtype)


def _head_matmul(a, w, out_dtype, name):
    M = a.shape[0]
    H, K, N = w.shape
    return pl.pallas_call(
        _headmm_body,
        grid=(H,),
        in_specs=[pl.BlockSpec((M, K), lambda h: (0, h)),
                  pl.BlockSpec((1, K, N), lambda h: (h, 0, 0))],
        out_specs=pl.BlockSpec((M, N), lambda h: (0, h)),
        out_shape=jax.ShapeDtypeStruct((M, H * N), out_dtype),
        compiler_params=_cp(("parallel",)),
        name=name,
    )(a, w)


def _gdn_body(cin_ref, z_ref, sm_ref, cw_ref, cs_ref, s0_ref, al_ref, dtb_ref, nw_ref,
              o_ref, sout_ref, xp, S, smp, *, NB, C, Lb, H, DK, DV, KC):
    c = pl.program_id(1)
    last = pl.num_programs(1) - 1
    HIST = KC - 1
    R0 = SUBLANES

    @pl.when(c == 0)
    def _init():
        xp[...] = jnp.zeros_like(xp)
        for n in range(NB):
            xp[n, R0 - HIST:R0, :] = cs_ref[n]
        S[...] = s0_ref[...]
        smp[...] = jnp.zeros_like(smp)

    for n in range(NB):
        xp[n, R0:R0 + Lb, :] = cin_ref[n]
        smp[n, 0:Lb, :] = sm_ref[n]

    lane = lax.broadcasted_iota(jnp.int32, (C, LANES), 1)
    row = lax.broadcasted_iota(jnp.int32, (C, LANES), 0)
    valid = row < Lb
    ri = lax.broadcasted_iota(jnp.int32, (C, C), 0)
    ci = lax.broadcasted_iota(jnp.int32, (C, C), 1)
    tril = ri >= ci
    strict = ri > ci
    eye = (ri == ci).astype(F32)
    trilf = tril.astype(F32)
    sel = (lax.broadcasted_iota(jnp.int32, (SUBLANES, LANES), 1)
           == lax.broadcasted_iota(jnp.int32, (SUBLANES, LANES), 0) + H).astype(F32)

    gates = []
    for n in range(NB):
        smv = smp[n]
        beta = jnp.where(valid & (lane < H), _sigmoid(smv), 0.0)
        xa = smv + dtb_ref[...]
        softplus = jnp.maximum(xa, 0.0) + jnp.log(1.0 + jnp.exp(-jnp.abs(xa)))
        g = jnp.where(valid & (lane >= H) & (lane < 2 * H), -jnp.exp(al_ref[...]) * softplus, 0.0)
        gc = jnp.dot(trilf, g, precision=HP, preferred_element_type=F32)
        gr = lax.dot_general(sel, gc, NT, precision=HP, preferred_element_type=F32)
        gates.append((beta, gc, gr))

    def conv(n, off):
        acc = xp[n, R0 - HIST:R0 - HIST + C, off:off + LANES] * cw_ref[0:1, off:off + LANES]
        for j in range(1, KC):
            acc = acc + (xp[n, R0 - HIST + j:R0 - HIST + j + C, off:off + LANES]
                         * cw_ref[j:j + 1, off:off + LANES])
        return _silu(acc)

    chains = [(n, h) for n in range(NB) for h in range(H)]
    st = []
    for n, h in chains:
        beta, gc, gr = gates[n]
        q = conv(n, h * DK)
        k = conv(n, H * DK + h * DK)
        v = conv(n, 2 * H * DK + h * DV)
        q = q * lax.rsqrt(jnp.sum(q * q, axis=-1, keepdims=True) + EPS) * (DK ** -0.5)
        k = k * lax.rsqrt(jnp.sum(k * k, axis=-1, keepdims=True) + EPS)
        gcol = gc[:, H + h:H + h + 1]
        grow = gr[h:h + 1, :]
        bcol = beta[:, h:h + 1]
        glast = gc[C - 1:C, H + h:H + h + 1]
        decay = jnp.where(tril, jnp.exp(jnp.where(tril, gcol - grow, 0.0)), 0.0)
        kb = k * bcol
        egc = jnp.exp(gcol)
        st.append(dict(q=q, k=k, gcol=gcol, glast=glast, decay=decay, kb=kb,
                       rhs=jnp.concatenate([v * bcol, kb * egc], axis=1), qe=q * egc))

    for s in st:
        s["pw"] = -jnp.where(strict, _bdot(s["kb"], s["k"], NT) * s["decay"], 0.0)
        s["tinv"] = eye + s["pw"]
    n_levels = int(math.ceil(math.log2(C)))
    for s in st:
        s["pw"] = _bdot(s["pw"], s["pw"])
    for _ in range(n_levels - 2):
        for s in st:
            both = _bdot(jnp.concatenate([s["pw"], s["tinv"]], axis=0), s["pw"])
            s["pw"] = both[0:C]
            s["tinv"] = s["tinv"] + both[C:2 * C]
    for s in st:
        s["tinv"] = s["tinv"] + _bdot(s["tinv"], s["pw"])

    for s in st:
        sol = _bdot(s["tinv"], s["rhs"])
        s["u"] = sol[:, 0:DV]
        s["w"] = sol[:, DV:DV + DK]
        s["qk"] = _bdot(s["q"], s["k"], NT) * s["decay"]

    for (n, h), s in zip(chains, st):
        s["S"] = S[n, h]
        ws = _bdot(jnp.concatenate([s["w"], s["qe"]], axis=0), s["S"])
        s["v_new"] = s["u"] - ws[0:C]
        s["o"] = ws[C:2 * C]

    for (n, h), s in zip(chains, st):
        o = s["o"] + _bdot(s["qk"], s["v_new"])
        S[n, h] = (s["S"] * jnp.exp(s["glast"])
                   + _bdot(s["k"] * jnp.exp(s["glast"] - s["gcol"]), s["v_new"], TN))
        on = o * lax.rsqrt(jnp.mean(o * o, axis=-1, keepdims=True) + EPS) * nw_ref[...]
        zz = z_ref[n, :, h * DV:(h + 1) * DV]
        o_ref[n, :, h * DV:(h + 1) * DV] = (on[0:Lb] * _silu(zz)).astype(o_ref.dtype)

    if Lb == C:
        for n in range(NB):
            xp[n, R0 - HIST:R0, :] = xp[n, R0 + C - HIST:R0 + C, :]

    @pl.when(c == last)
    def _fin():
        sout_ref[...] = S[...]


def _gdn(main3, small3, conv_w, conv_state, s0, a_log, dt_bias, norm_w, *, NB, C, Lb, H, DK, DV,
         out_dtype):
    B, L, _ = main3.shape
    KC, CW = conv_w.shape
    nc = L // Lb
    assert nc == 1 or Lb == C
    assert B % NB == 0
    pad = lambda vec, off: jnp.zeros((1, LANES), F32).at[0, off:off + H].set(vec.astype(F32))
    body = functools.partial(_gdn_body, NB=NB, C=C, Lb=Lb, H=H, DK=DK, DV=DV, KC=KC)
    return pl.pallas_call(
        body,
        grid=(B // NB, nc),
        in_specs=[
            pl.BlockSpec((NB, Lb, CW), lambda b, c: (b, c, 0)),
            pl.BlockSpec((NB, Lb, H * DV), lambda b, c: (b, c, CW // (H * DV))),
            pl.BlockSpec((NB, Lb, LANES), lambda b, c: (b, c, 2)),
            pl.BlockSpec((KC, CW), lambda b, c: (0, 0)),
            pl.BlockSpec((NB, KC - 1, CW), lambda b, c: (b, 0, 0)),
            pl.BlockSpec((NB, H, DK, DV), lambda b, c: (b, 0, 0, 0)),
            pl.BlockSpec((1, LANES), lambda b, c: (0, 0)),
            pl.BlockSpec((1, LANES), lambda b, c: (0, 0)),
            pl.BlockSpec((1, DV), lambda b, c: (0, 0)),
        ],
        out_specs=[
            pl.BlockSpec((NB, Lb, H * DV), lambda b, c: (b, c, 0)),
            pl.BlockSpec((NB, H, DK, DV), lambda b, c: (b, 0, 0, 0)),
        ],
        out_shape=[jax.ShapeDtypeStruct((B, L, H * DV), out_dtype),
                   jax.ShapeDtypeStruct((B, H, DK, DV), F32)],
        scratch_shapes=[pltpu.VMEM((NB, SUBLANES + C, CW), F32),
                        pltpu.VMEM((NB, H, DK, DV), F32),
                        pltpu.VMEM((NB, C, LANES), F32)],
        compiler_params=_cp(("parallel", "arbitrary")),
        name="gdn",
    )(main3, main3, small3, conv_w, conv_state, s0, pad(a_log, H), pad(dt_bias, H),
      norm_w.reshape(1, DV).astype(F32))


def _mlaproj_body(cq_ref, ckv_ref, sm_ref, cos_ref, sin_ref, wq_ref, gq_ref, gkv_ref,
                  qn_ref, qr_ref, ckv_o_ref, ckvb_ref, kr_o_ref, krb_ref, *, H, NOPE, ROPE, scale):
    cq = cq_ref[...]
    cq = (cq * lax.rsqrt(jnp.mean(cq * cq, axis=-1, keepdims=True) + EPS) * gq_ref[...]).astype(BF16)
    q = jnp.dot(cq, wq_ref[...], preferred_element_type=F32)
    HW = H * LANES
    cosp = cos_ref[...]
    sinp = sin_ref[...]
    cos_h = jnp.concatenate([cosp] * H, axis=1)
    sin_h = jnp.concatenate([sinp] * H, axis=1)
    qn_ref[...] = (q[:, 0:H * NOPE] * scale).astype(qn_ref.dtype)
    qr = q[:, H * NOPE:H * NOPE + HW] * cos_h + q[:, H * NOPE + HW:H * NOPE + 2 * HW] * sin_h
    qr_ref[...] = (qr * scale).astype(qr_ref.dtype)
    ckv = ckv_ref[...]
    ckv = ckv * lax.rsqrt(jnp.mean(ckv * ckv, axis=-1, keepdims=True) + EPS) * gkv_ref[...]
    ckv_o_ref[...] = ckv
    ckvb_ref[...] = ckv.astype(BF16)
    sm = sm_ref[...]
    kr = sm[:, 0:LANES] * cosp + sm[:, LANES:2 * LANES] * sinp
    kr_o_ref[...] = kr[:, 0:ROPE]
    krb_ref[...] = kr.astype(BF16)


def _mla_proj(main2, small2, cosp, sinp, wq, g_q, g_kv, *, tl, H, NOPE, ROPE, QL, KVL, scale, table_blocks):
    T, W = main2.shape
    cq_blk = (W - QL - KVL) // QL
    assert QL == KVL
    nt = table_blocks
    body = functools.partial(_mlaproj_body, H=H, NOPE=NOPE, ROPE=ROPE, scale=scale)
    return pl.pallas_call(
        body,
        grid=(T // tl,),
        in_specs=[
            pl.BlockSpec((tl, QL), lambda i: (i, cq_blk)),
            pl.BlockSpec((tl, KVL), lambda i: (i, cq_blk + 1)),
            pl.BlockSpec((tl, 2 * LANES), lambda i: (i, 0)),
            pl.BlockSpec((tl, LANES), lambda i: (i % nt, 0)),
            pl.BlockSpec((tl, LANES), lambda i: (i % nt, 0)),
            pl.BlockSpec(wq.shape, lambda i: (0, 0)),
            pl.BlockSpec((1, QL), lambda i: (0, 0)),
            pl.BlockSpec((1, KVL), lambda i: (0, 0)),
        ],
        out_specs=[
            pl.BlockSpec((tl, H * NOPE), lambda i: (i, 0)),
            pl.BlockSpec((tl, H * LANES), lambda i: (i, 0)),
            pl.BlockSpec((tl, KVL), lambda i: (i, 0)),
            pl.BlockSpec((tl, KVL), lambda i: (i, 0)),
            pl.BlockSpec((tl, ROPE), lambda i: (i, 0)),
            pl.BlockSpec((tl, LANES), lambda i: (i, 0)),
        ],
        out_shape=[
            jax.ShapeDtypeStruct((T, H * NOPE), BF16),
            jax.ShapeDtypeStruct((T, H * LANES), BF16),
            jax.ShapeDtypeStruct((T, KVL), F32),
            jax.ShapeDtypeStruct((T, KVL), BF16),
            jax.ShapeDtypeStruct((T, ROPE), F32),
            jax.ShapeDtypeStruct((T, LANES), BF16),
        ],
        compiler_params=_cp(("parallel",)),
        name="mla_proj",
    )(main2, main2, small2, cosp, sinp, wq, g_q.reshape(1, QL), g_kv.reshape(1, KVL))


def _pattn_body(qn_ref, qr_ref, k_ref, kr_ref, wuk_ref, wuv_ref, o_ref,
                ql_s, qr_s, m_s, l_s, acc_s, *, H, tq, tk, NOPE, VH):
    i = pl.program_id(1)
    for h in range(H):
        ql_s[h * tq:(h + 1) * tq, :] = jnp.dot(
            qn_ref[0, :, h * NOPE:(h + 1) * NOPE], wuk_ref[h], preferred_element_type=F32).astype(BF16)
        qr_s[h * tq:(h + 1) * tq, :] = qr_ref[0, :, h * LANES:(h + 1) * LANES]
    m_s[...] = jnp.full_like(m_s, NEG)
    l_s[...] = jnp.zeros_like(l_s)
    acc_s[...] = jnp.zeros_like(acc_s)
    R = H * tq
    qpos = i * tq + (lax.broadcasted_iota(jnp.int32, (R, tk), 0) & (tq - 1))
    kofs = lax.broadcasted_iota(jnp.int32, (R, tk), 1)
    nkv = (i * tq + tq + tk - 1) // tk

    def scores(j):
        start = pl.multiple_of(j * tk, tk)
        return (lax.dot_general(ql_s[...], k_ref[0, pl.ds(start, tk), :], NT,
                                preferred_element_type=F32)
                + lax.dot_general(qr_s[...], kr_ref[0, pl.ds(start, tk), :], NT,
                                  preferred_element_type=F32))

    def absorb(s, j, masked):
        start = pl.multiple_of(j * tk, tk)
        if masked:
            s = jnp.where(kofs + j * tk <= qpos, s, NEG)
        m_prev = m_s[...]
        m_new = jnp.maximum(m_prev, jnp.max(s, axis=-1, keepdims=True))
        alpha = jnp.exp(m_prev - m_new)
        p = jnp.exp(s - m_new)
        l_s[...] = alpha * l_s[...] + jnp.sum(p, axis=-1, keepdims=True)
        acc_s[...] = alpha * acc_s[...] + jnp.dot(p.astype(BF16), k_ref[0, pl.ds(start, tk), :],
                                                  preferred_element_type=F32)
        m_s[...] = m_new

    def full_chunk(j, s_cur):
        s_next = scores(j + 1)
        absorb(s_cur, j, False)
        return s_next

    s_last = lax.fori_loop(0, nkv - 1, full_chunk, scores(0))
    absorb(s_last, nkv - 1, True)
    o = (acc_s[...] / l_s[...]).astype(BF16)
    for h in range(H):
        o_ref[0, :, h * VH:(h + 1) * VH] = jnp.dot(
            o[h * tq:(h + 1) * tq, :], wuv_ref[h], preferred_element_type=F32).astype(o_ref.dtype)


def _prompt_attention(qn, qr, ckvb, krb, wuk_t, wuv, *, tq, tk):
    B, L, _ = qn.shape
    H, NOPE, KVL = wuk_t.shape
    VH = wuv.shape[2]
    assert tq & (tq - 1) == 0 and tk % tq == 0 and H % 2 == 0
    body = functools.partial(_pattn_body, H=H, tq=tq, tk=tk, NOPE=NOPE, VH=VH)
    return pl.pallas_call(
        body,
        grid=(B, L // tq),
        in_specs=[
            pl.BlockSpec((1, tq, H * NOPE), lambda b, i: (b, i, 0)),
            pl.BlockSpec((1, tq, H * LANES), lambda b, i: (b, i, 0)),
            pl.BlockSpec((1, L, KVL), lambda b, i: (b, 0, 0)),
            pl.BlockSpec((1, L, LANES), lambda b, i: (b, 0, 0)),
            pl.BlockSpec((H, NOPE, KVL), lambda b, i: (0, 0, 0)),
            pl.BlockSpec((H, KVL, VH), lambda b, i: (0, 0, 0)),
        ],
        out_specs=pl.BlockSpec((1, tq, H * VH), lambda b, i: (b, i, 0)),
        out_shape=jax.ShapeDtypeStruct((B, L, H * VH), BF16),
        scratch_shapes=[pltpu.VMEM((H * tq, KVL), BF16),
                        pltpu.VMEM((H * tq, LANES), BF16),
                        pltpu.VMEM((H * tq, 1), F32),
                        pltpu.VMEM((H * tq, 1), F32),
                        pltpu.VMEM((H * tq, KVL), F32)],
        compiler_params=_cp(("parallel", "arbitrary")),
        name="prompt_attn",
    )(qn, qr, ckvb, krb, wuk_t, wuv)


def _sattn_body(pt_ref, ql_ref, qr_ref, cn_ref, krn_ref, cc_hbm, ckt_hbm, o_ref,
                kbuf, rbuf, sem, m_s, l_s, acc_s, knew, rnew, *, P, NC, PS, ROPE, H, T, layer):
    b = pl.program_id(0)
    c = pl.program_id(1)
    step = b * NC + c
    nsteps = pl.num_programs(0) * NC
    slot = step % 2

    def page_copies(stp, sl):
        bb = stp // NC
        cc = stp % NC
        cps = []
        for p in range(P):
            pg = pt_ref[bb, cc * P + p]
            cps.append(pltpu.make_async_copy(cc_hbm.at[layer, pg], kbuf.at[sl, p], sem.at[0, sl]))
            cps.append(pltpu.make_async_copy(ckt_hbm.at[layer, pg], rbuf.at[sl, p], sem.at[1, sl]))
        return cps

    @pl.when(step == 0)
    def _first():
        for cp in page_copies(0, 0):
            cp.start()

    @pl.when(step + 1 < nsteps)
    def _prefetch():
        for cp in page_copies(step + 1, 1 - slot):
            cp.start()

    for cp in page_copies(step, slot):
        cp.wait()

    @pl.when(c == 0)
    def _init():
        m_s[...] = jnp.full_like(m_s, NEG)
        l_s[...] = jnp.zeros_like(l_s)
        acc_s[...] = jnp.zeros_like(acc_s)

    ql = ql_ref[0]
    qr = qr_ref[0]

    def update(s, vals):
        m_prev = m_s[...]
        m_new = jnp.maximum(m_prev, jnp.max(s, axis=-1, keepdims=True))
        alpha = jnp.exp(m_prev - m_new)
        p = jnp.exp(s - m_new)
        l_s[...] = alpha * l_s[...] + jnp.sum(p, axis=-1, keepdims=True)
        acc_s[...] = alpha * acc_s[...] + jnp.dot(p.astype(BF16), vals, preferred_element_type=F32)
        m_s[...] = m_new

    k = kbuf[slot].reshape(P * PS, kbuf.shape[-1]).astype(BF16)
    qr_rope = qr[:, 0:ROPE]
    s_rope = jnp.concatenate(
        [jnp.dot(qr_rope, rbuf[slot, p].astype(BF16), preferred_element_type=F32) for p in range(P)],
        axis=1)
    s = lax.dot_general(ql, k, NT, preferred_element_type=F32) + s_rope
    update(s, k)

    @pl.when(c == NC - 1)
    def _fin():
        knew[...] = jnp.zeros_like(knew)
        rnew[...] = jnp.zeros_like(rnew)
        knew[0:T, :] = cn_ref[0]
        rnew[0:T, :] = krn_ref[0]
        kn = knew[...].astype(BF16)
        rn = rnew[...].astype(BF16)
        s2 = (lax.dot_general(ql, kn, NT, preferred_element_type=F32)
              + lax.dot_general(qr, rn, NT, preferred_element_type=F32))
        R = T * H
        trow = lax.broadcasted_iota(jnp.int32, (R, LANES), 0) // H
        kcol = lax.broadcasted_iota(jnp.int32, (R, LANES), 1)
        s2 = jnp.where(kcol <= trow, s2, NEG)
        update(s2, kn)
        o_ref[0] = acc_s[...] / l_s[...]


def _sample_attention(page_table, ql, qr, cn, krn, cache_ckv, cache_krt, *, P, H, T, layer):
    DB, NPG = page_table.shape
    _, _, PS, KVL = cache_ckv.shape
    ROPE = cache_krt.shape[2]
    NC = NPG // P
    R = T * H
    body = functools.partial(_sattn_body, P=P, NC=NC, PS=PS, ROPE=ROPE, H=H, T=T, layer=layer)
    grid_spec = pltpu.PrefetchScalarGridSpec(
        num_scalar_prefetch=1,
        grid=(DB, NC),
        in_specs=[
            pl.BlockSpec((1, R, KVL), lambda b, c, pt: (b, 0, 0)),
            pl.BlockSpec((1, R, LANES), lambda b, c, pt: (b, 0, 0)),
            pl.BlockSpec((1, T, KVL), lambda b, c, pt: (b, 0, 0)),
            pl.BlockSpec((1, T, LANES), lambda b, c, pt: (b, 0, 0)),
            pl.BlockSpec(memory_space=pl.ANY),
            pl.BlockSpec(memory_space=pl.ANY),
        ],
        out_specs=pl.BlockSpec((1, R, KVL), lambda b, c, pt: (b, 0, 0)),
        scratch_shapes=[pltpu.VMEM((2, P, PS, KVL), F32),
                        pltpu.VMEM((2, P, ROPE, PS), F32),
                        pltpu.SemaphoreType.DMA((2, 2)),
                        pltpu.VMEM((R, 1), F32),
                        pltpu.VMEM((R, 1), F32),
                        pltpu.VMEM((R, KVL), F32),
                        pltpu.VMEM((LANES, KVL), F32),
                        pltpu.VMEM((LANES, LANES), F32)],
    )
    return pl.pallas_call(
        body,
        grid_spec=grid_spec,
        out_shape=jax.ShapeDtypeStruct((DB, R, KVL), F32),
        compiler_params=_cp(("arbitrary", "arbitrary")),
        name="sample_attn",
    )(page_table, ql, qr, cn, krn, cache_ckv, cache_krt)


class _PageStream:
    def __init__(self, page_table, ql, qr, cn, krn, cache_ckv, cache_krt, *, P, H, T, layer):
        self.nseq, npg = page_table.shape
        _, _, self.PS, self.KVL = cache_ckv.shape
        self.ROPE = cache_krt.shape[2]
        self.P, self.H, self.T, self.layer = P, H, T, layer
        self.NC = npg // P
        assert self.NC % self.SLOTS == 0
        self.R = T * H
        self.page_table = page_table
        self.args = (ql, qr, cn, krn, cache_ckv, cache_krt)

    def in_specs(self, step_of):
        seq = lambda *g: (jnp.minimum(step_of(*g[:-1]), self.nseq - 1), 0, 0)
        return [pl.BlockSpec((1, self.R, self.KVL), seq),
                pl.BlockSpec((1, self.R, LANES), seq),
                pl.BlockSpec((1, self.T, self.KVL), seq),
                pl.BlockSpec((1, self.T, LANES), seq),
                pl.BlockSpec(memory_space=pl.ANY),
                pl.BlockSpec(memory_space=pl.ANY)], pl.BlockSpec((1, self.R, self.KVL), seq)

    def out_shape(self):
        return jax.ShapeDtypeStruct((self.nseq, self.R, self.KVL), F32)

    def scratch_shapes(self):
        return [pltpu.VMEM((self.SLOTS, self.P, self.PS, self.KVL), F32),
                pltpu.VMEM((self.SLOTS, self.P, self.ROPE, self.PS), F32),
                pltpu.SemaphoreType.DMA((2, self.SLOTS)),
                pltpu.VMEM((self.R, 1), F32),
                pltpu.VMEM((self.R, 1), F32),
                pltpu.VMEM((self.R, self.KVL), F32),
                pltpu.VMEM((LANES, self.KVL), F32),
                pltpu.VMEM((LANES, LANES), F32)]

    N_IN = 6
    SLOTS = 4
    DEPTH = 3

    def bind(self, step, total_steps, pt_ref, in_refs, o_ref, scratch):
        ql_ref, qr_ref, cn_ref, krn_ref, cc_hbm, ckt_hbm = in_refs
        kbuf, rbuf, sem, m_s, l_s, acc_s, knew, rnew = scratch
        P, NC, PS, ROPE, H, T, layer, nseq = (self.P, self.NC, self.PS, self.ROPE, self.H,
                                              self.T, self.layer, self.nseq)

        def when_active(fn):
            if nseq >= total_steps:
                fn()
            else:
                pl.when(step < nseq)(fn)

        def copies(seq, c, sl):
            cps = []
            for p in range(P):
                pg = pt_ref[seq, c * P + p]
                cps.append(pltpu.make_async_copy(cc_hbm.at[layer, pg], kbuf.at[sl, p], sem.at[0, sl]))
                cps.append(pltpu.make_async_copy(ckt_hbm.at[layer, pg], rbuf.at[sl, p], sem.at[1, sl]))
            return cps

        def start(seq, c, sl):
            for cp in copies(seq, c, sl):
                cp.start()

        def update(s, vals):
            m_prev = m_s[...]
            m_new = jnp.maximum(m_prev, jnp.max(s, axis=-1, keepdims=True))
            alpha = jnp.exp(m_prev - m_new)
            p = jnp.exp(s - m_new)
            l_s[...] = alpha * l_s[...] + jnp.sum(p, axis=-1, keepdims=True)
            acc_s[...] = alpha * acc_s[...] + jnp.dot(p.astype(BF16), vals, preferred_element_type=F32)
            m_s[...] = m_new

        S, D = self.SLOTS, self.DEPTH

        def pre(u):
            def go():
                if u == 0:
                    def prime():
                        for c in range(D):
                            start(0, c, c % S)
                    pl.when(step == 0)(prime)
                    m_s[...] = jnp.full_like(m_s, NEG)
                    l_s[...] = jnp.zeros_like(l_s)
                    acc_s[...] = jnp.zeros_like(acc_s)
                c = u + D
                if c < NC:
                    start(step, c, c % S)
                else:
                    pl.when(step + 1 < nseq)(lambda: start(step + 1, c - NC, (c - NC) % S))
            when_active(go)

        def post(u):
            def go():
                sl = u % S
                for cp in copies(step, u, sl):
                    cp.wait()
                ql = ql_ref[0]
                qr = qr_ref[0]
                k = kbuf[sl].reshape(P * PS, self.KVL).astype(BF16)
                qr_rope = qr[:, 0:ROPE]
                s_rope = jnp.concatenate(
                    [jnp.dot(qr_rope, rbuf[sl, p].astype(BF16), preferred_element_type=F32)
                     for p in range(P)], axis=1)
                update(lax.dot_general(ql, k, NT, preferred_element_type=F32) + s_rope, k)
                if u == NC - 1:
                    knew[...] = jnp.zeros_like(knew)
                    rnew[...] = jnp.zeros_like(rnew)
                    knew[0:T, :] = cn_ref[0]
                    rnew[0:T, :] = krn_ref[0]
                    kn = knew[...].astype(BF16)
                    rn = rnew[...].astype(BF16)
                    s2 = (lax.dot_general(ql, kn, NT, preferred_element_type=F32)
                          + lax.dot_general(qr, rn, NT, preferred_element_type=F32))
                    trow = lax.broadcasted_iota(jnp.int32, (T * H, LANES), 0) // H
                    kcol = lax.broadcasted_iota(jnp.int32, (T * H, LANES), 1)
                    update(jnp.where(kcol <= trow, s2, NEG), kn)
                    o_ref[0] = acc_s[...] / l_s[...]
            when_active(go)

        return pre, post


def _inproj_stream_body(*refs, stream, n_steps_i, tm):
    pt_ref, a_ref, w_ref = refs[0:3]
    s_in = refs[3:3 + stream.N_IN]
    o_ref, so_ref = refs[3 + stream.N_IN:5 + stream.N_IN]
    s_scr = refs[5 + stream.N_IN:]
    step = pl.program_id(0) * n_steps_i + pl.program_id(1)
    total = pl.num_programs(0) * n_steps_i
    pre, post = stream.bind(step, total, pt_ref, s_in, so_ref, s_scr)
    rows = tm // stream.NC
    for u in range(stream.NC):
        pre(u)
        o_ref[u * rows:(u + 1) * rows, :] = jnp.dot(
            a_ref[u * rows:(u + 1) * rows, :], w_ref[...], preferred_element_type=F32
        ).astype(o_ref.dtype)
        post(u)


def _matmul_with_stream(a, w, out_dtype, tm, tn, name, stream):
    M, K = a.shape
    N = w.shape[1]
    nj, ni = N // tn, M // tm
    assert stream.nseq <= nj * ni and tm % (stream.NC * SUBLANES * 2) == 0
    step_of = lambda j, i: j * ni + i
    s_in_specs, s_out_spec = stream.in_specs(step_of)
    grid_spec = pltpu.PrefetchScalarGridSpec(
        num_scalar_prefetch=1,
        grid=(nj, ni),
        in_specs=[pl.BlockSpec((tm, K), lambda j, i, pt: (i, 0)),
                  pl.BlockSpec((K, tn), lambda j, i, pt: (0, j))] + s_in_specs,
        out_specs=[pl.BlockSpec((tm, tn), lambda j, i, pt: (i, j)), s_out_spec],
        scratch_shapes=stream.scratch_shapes(),
    )
    return pl.pallas_call(
        functools.partial(_inproj_stream_body, stream=stream, n_steps_i=ni, tm=tm),
        grid_spec=grid_spec,
        out_shape=[jax.ShapeDtypeStruct((M, N), out_dtype), stream.out_shape()],
        compiler_params=_cp(("arbitrary", "arbitrary")),
        name=name,
    )(stream.page_table, a, w, *stream.args)


def _upconv_body(*refs, tm, KC, tiles_per_seq, stream, n_steps_i):
    if stream is None:
        (a_ref, wg_ref, wv_ref, cwg_ref, cwv_ref, sg_ref, sv_ref,
         act_ref, ng_ref, nv_ref, xg, xv) = refs
        n_slices = 1
        pre = post = lambda u: None
    else:
        pt_ref = refs[0]
        a_ref, wg_ref, wv_ref, cwg_ref, cwv_ref, sg_ref, sv_ref = refs[1:8]
        s_in = refs[8:8 + stream.N_IN]
        act_ref, ng_ref, nv_ref, so_ref = refs[8 + stream.N_IN:12 + stream.N_IN]
        xg, xv = refs[12 + stream.N_IN:14 + stream.N_IN]
        s_scr = refs[14 + stream.N_IN:]
        n_slices = stream.NC
        step = pl.program_id(0) * n_steps_i + pl.program_id(1)
        total = pl.num_programs(0) * n_steps_i
        pre, post = stream.bind(step, total, pt_ref, s_in, so_ref, s_scr)
    i = pl.program_id(1)
    HIST = KC - 1
    R0 = SUBLANES

    @pl.when(i % tiles_per_seq == 0)
    def _init():
        xg[R0 - HIST:R0, :] = sg_ref[0]
        xv[R0 - HIST:R0, :] = sv_ref[0]

    rows = tm // n_slices
    for u in range(n_slices):
        pre(u)
        a = a_ref[u * rows:(u + 1) * rows, :]
        xg[R0 + u * rows:R0 + (u + 1) * rows, :] = jnp.dot(a, wg_ref[...], preferred_element_type=F32)
        xv[R0 + u * rows:R0 + (u + 1) * rows, :] = jnp.dot(a, wv_ref[...], preferred_element_type=F32)
        post(u)

    def conv(x, w_ref):
        acc = x[R0 - HIST:R0 - HIST + tm, :] * w_ref[0:1, :]
        for j in range(1, KC):
            acc = acc + x[R0 - HIST + j:R0 - HIST + j + tm, :] * w_ref[j:j + 1, :]
        return acc

    act_ref[...] = (_silu(conv(xg, cwg_ref)) * conv(xv, cwv_ref)).astype(act_ref.dtype)
    tail_g = xg[R0 + tm - HIST:R0 + tm, :]
    tail_v = xv[R0 + tm - HIST:R0 + tm, :]
    ng_ref[0] = tail_g
    nv_ref[0] = tail_v
    xg[R0 - HIST:R0, :] = tail_g
    xv[R0 - HIST:R0, :] = tail_v


def _up_conv_act(h, w_up, conv_w, state, *, L, tm, tn, stream=None):
    T, D = h.shape
    F2 = w_up.shape[1]
    F = F2 // 2
    B = T // L
    KC = conv_w.shape[0]
    nv = F // tn
    tps = L // tm
    nj, ni = F // tn, T // tm
    body = functools.partial(_upconv_body, tm=tm, KC=KC, tiles_per_seq=tps, stream=stream,
                             n_steps_i=ni)
    in_specs = [
        pl.BlockSpec((tm, D), lambda j, i, *_: (i, 0)),
        pl.BlockSpec((D, tn), lambda j, i, *_: (0, j)),
        pl.BlockSpec((D, tn), lambda j, i, *_: (0, j + nv)),
        pl.BlockSpec((KC, tn), lambda j, i, *_: (0, j)),
        pl.BlockSpec((KC, tn), lambda j, i, *_: (0, j + nv)),
        pl.BlockSpec((1, KC - 1, tn), lambda j, i, *_: (i // tps, 0, j)),
        pl.BlockSpec((1, KC - 1, tn), lambda j, i, *_: (i // tps, 0, j + nv)),
    ]
    out_specs = [
        pl.BlockSpec((tm, tn), lambda j, i, *_: (i, j)),
        pl.BlockSpec((1, KC - 1, tn), lambda j, i, *_: (i // tps, 0, j)),
        pl.BlockSpec((1, KC - 1, tn), lambda j, i, *_: (i // tps, 0, j)),
    ]
    out_shape = [jax.ShapeDtypeStruct((T, F), BF16),
                 jax.ShapeDtypeStruct((B, KC - 1, F), F32),
                 jax.ShapeDtypeStruct((B, KC - 1, F), F32)]
    scratch = [pltpu.VMEM((SUBLANES + tm, tn), F32), pltpu.VMEM((SUBLANES + tm, tn), F32)]
    args = (h, w_up, w_up, conv_w, conv_w, state, state)
    if stream is None:
        act, ng, nvv = pl.pallas_call(
            body, grid=(nj, ni), in_specs=in_specs, out_specs=out_specs, out_shape=out_shape,
            scratch_shapes=scratch, compiler_params=_cp(("parallel", "arbitrary")),
            name="up_conv_act",
        )(*args)
        return act, jnp.concatenate([ng, nvv], axis=2), None
    assert stream.nseq <= nj * ni and tm % (stream.NC * SUBLANES * 2) == 0
    s_in_specs, s_out_spec = stream.in_specs(lambda j, i: j * ni + i)
    grid_spec = pltpu.PrefetchScalarGridSpec(
        num_scalar_prefetch=1, grid=(nj, ni), in_specs=in_specs + s_in_specs,
        out_specs=out_specs + [s_out_spec], scratch_shapes=scratch + stream.scratch_shapes())
    act, ng, nvv, s_out = pl.pallas_call(
        body, grid_spec=grid_spec, out_shape=out_shape + [stream.out_shape()],
        compiler_params=_cp(("arbitrary", "arbitrary")), name="up_conv_act",
    )(stream.page_table, *args, *stream.args)
    return act, jnp.concatenate([ng, nvv], axis=2), s_out


def _ffnconv_tm_body(xg_ref, xv_ref, wg_ref, wv_ref, o_ref, *, T, KC):
    for t in range(T):
        g = xg_ref[t] * wg_ref[0:1, :]
        v = xv_ref[t] * wv_ref[0:1, :]
        for j in range(1, KC):
            g = g + xg_ref[t + j] * wg_ref[j:j + 1, :]
            v = v + xv_ref[t + j] * wv_ref[j:j + 1, :]
        o_ref[t] = (_silu(g) * v).astype(o_ref.dtype)


def _ffn_conv_timemajor(xp, conv_w, *, tc):
    TT, NB, F2 = xp.shape
    F = F2 // 2
    KC = conv_w.shape[0]
    T = TT - (KC - 1)
    nv = F // tc
    body = functools.partial(_ffnconv_tm_body, T=T, KC=KC)
    return pl.pallas_call(
        body,
        grid=(F // tc,),
        in_specs=[
            pl.BlockSpec((TT, NB, tc), lambda j: (0, 0, j)),
            pl.BlockSpec((TT, NB, tc), lambda j: (0, 0, j + nv)),
            pl.BlockSpec((KC, tc), lambda j: (0, j)),
            pl.BlockSpec((KC, tc), lambda j: (0, j + nv)),
        ],
        out_specs=pl.BlockSpec((T, NB, tc), lambda j: (0, 0, j)),
        out_shape=jax.ShapeDtypeStruct((T, NB, F), BF16),
        compiler_params=_cp(("parallel",)),
        name="ffn_conv_tm",
    )(xp, xp, conv_w, conv_w)


def _prep_weights(w_in, w_uq, w_uk, w_uv, *, H, DK, DV, QL, KVL, NOPE, ROPE, VH, D):
    conv_ch = 2 * H * DK + H * DV
    sizes = (conv_ch, H * DV, H, H, QL, KVL, ROPE, D, D)
    offs = np.cumsum((0,) + sizes)
    seg = lambda n: w_in[:, offs[n]:offs[n + 1]]
    w_conv, w_z, w_b, w_a, w_cq, w_ckv, w_kr, w_ga, w_gb = (seg(n) for n in range(9))
    half = ROPE // 2
    w_main = jnp.concatenate([w_conv, w_z, w_ga, w_gb, w_cq, w_ckv], axis=1).astype(BF16)
    zc = lambda n: jnp.zeros((D, n), w_in.dtype)
    w_kr_sw = jnp.concatenate([w_kr[:, half:], w_kr[:, :half]], axis=1)
    w_small = jnp.concatenate([w_kr, zc(LANES - ROPE), w_kr_sw, zc(LANES - ROPE),
                               w_b, w_a, zc(LANES - 2 * H)], axis=1).astype(BF16)
    wq = w_uq.reshape(QL, H, NOPE + ROPE)
    wq_n = wq[:, :, :NOPE].reshape(QL, H * NOPE)
    wq_r = wq[:, :, NOPE:]
    zr = jnp.zeros((QL, H, LANES - ROPE), w_uq.dtype)
    wq_rp = jnp.concatenate([wq_r, zr], axis=2).reshape(QL, H * LANES)
    wq_rs = jnp.concatenate([wq_r[:, :, half:], wq_r[:, :, :half], zr], axis=2).reshape(QL, H * LANES)
    wq_all = jnp.concatenate([wq_n, wq_rp, wq_rs], axis=1).astype(BF16)
    wuk_t = jnp.transpose(w_uk, (1, 2, 0)).astype(BF16)
    wuv = jnp.transpose(w_uv, (1, 0, 2)).astype(BF16)
    return w_main, w_small, wq_all, wuk_t, wuv


def _rope_tables(pos, rope):
    half = rope // 2
    inv = np.exp(-math.log(ROPE_BASE) * np.arange(half, dtype=np.float64) / half)
    ang = np.asarray(pos).astype(np.float64)[:, None] * inv[None, :]
    c, s = np.cos(ang).astype(np.float32), np.sin(ang).astype(np.float32)
    z = np.zeros((ang.shape[0], LANES - rope), np.float32)
    return np.concatenate([c, c, z], axis=1), np.concatenate([-s, s, z], axis=1)


def _last_rows(hist, x):
    n, L = hist.shape[1], x.shape[1]
    if L >= n:
        return x[:, L - n:]
    return jnp.concatenate([hist[:, L:], x], axis=1)


def kernel(x_prompt, x_sample, cache_ckv, cache_krope, page_table, state_gdn, state_gdn_conv,
           state_ffn_conv, p_prompt, p_sample, g_mix, w_in, gdn_conv_w, gdn_A_log, gdn_dt_bias,
           gdn_norm_w, w_o_gdn, g_q_a, w_uq, g_kv_a, w_uk, w_uv, w_o_mla, w_out, g_ffn, w_up,
           ffn_conv_w, w_down, g_ple, w_ple_gate, w_ple_proj, g_final):
    BP, LP, D = x_prompt.shape
    DB, TS, _ = x_sample.shape
    depth = w_in.shape[0]
    H, DK, DV = state_gdn.shape[2], state_gdn.shape[3], state_gdn.shape[4]
    KC = gdn_conv_w.shape[1]
    CW = gdn_conv_w.shape[2]
    QL = g_q_a.shape[1]
    KVL, MH, NOPE = w_uk.shape[1], w_uk.shape[2], w_uk.shape[3]
    VH = w_uv.shape[3]
    ROPE = cache_krope.shape[3]
    PS = cache_ckv.shape[2]
    NPG = page_table.shape[1]
    F2 = w_up.shape[2]
    F = F2 // 2
    FK = ffn_conv_w.shape[1]
    assert H == MH
    scale = (NOPE + ROPE) ** -0.5
    past_len = NPG * PS
    W_MAIN = CW + H * DV + 2 * D + QL + KVL

    cos_p, sin_p = _rope_tables(np.arange(LP), ROPE)
    cos_s, sin_s = _rope_tables(past_len + np.arange(TS), ROPE)
    cos_s = np.tile(cos_s, (DB, 1))
    sin_s = np.tile(sin_s, (DB, 1))
    cache_krt = jnp.swapaxes(cache_krope, 2, 3)

    xp = x_prompt.reshape(BP * LP, D)
    xs = x_sample.reshape(DB * TS, D)
    outs = {k: [] for k in ("ckv_p", "kr_p", "ckv_s", "kr_s", "gdn_p", "gdn_s",
                            "gc_p", "gc_s", "fc_p", "fc_s")}

    for l in range(depth):
        w_main, w_small, wq_all, wuk_t, wuv = _prep_weights(
            w_in[l], w_uq[l], w_uk[l], w_uv[l], H=H, DK=DK, DV=DV, QL=QL, KVL=KVL,
            NOPE=NOPE, ROPE=ROPE, VH=VH, D=D)
        wo_gdn = w_o_gdn[l].astype(BF16)
        wo_mla = w_o_mla[l].astype(BF16)
        wout = w_out[l].astype(BF16)
        wup = w_up[l].astype(BF16)
        wdown = w_down[l].astype(BF16)
        wpg = w_ple_gate[l].astype(BF16)
        wpp = w_ple_proj[l].astype(BF16)

        TN = 1024
        TN_UP = 512

        def tiles(T):
            return min(512, T), min(1024, T)

        def front(x, *, prompt, stream):
            T = x.shape[0]
            tr, tm = tiles(T)
            tn = TN
            h = _rmsnorm(x, g_mix[l], BF16, tr)
            if stream is None:
                main = _matmul([(h, w_main)], [], _epi_plain, F32, tm, tn, "in_proj")
                stream_out = None
            else:
                main, stream_out = _matmul_with_stream(h, w_main, F32, tm, tn, "in_proj", stream)
            small = _matmul([(h, w_small)], [], _epi_plain, F32, tr, 3 * LANES, "in_proj_small")
            if prompt:
                nb, L = BP, LP
                conv_state = jnp.zeros((nb, KC - 1, CW), F32)
                s0 = jnp.zeros((nb, H, DK, DV), F32)
                cos_t, sin_t, chunk = cos_p, sin_p, 64
            else:
                nb, L = DB, TS
                conv_state = state_gdn_conv[l]
                s0 = state_gdn[l]
                cos_t, sin_t, chunk = cos_s, sin_s, SUBLANES
            tl = min(256, T)
            tblk = cos_t.shape[0] // tl
            main3 = main.reshape(nb, L, W_MAIN)
            small3 = small.reshape(nb, L, 3 * LANES)
            o_gdn, s_new = _gdn(main3, small3, gdn_conv_w[l], conv_state, s0, gdn_A_log[l],
                                gdn_dt_bias[l], gdn_norm_w[l], NB=2, C=chunk, Lb=min(chunk, L),
                                H=H, DK=DK, DV=DV, out_dtype=BF16 if prompt else F32)
            o_gdn = o_gdn.reshape(T, H * DV).astype(BF16)
            new_gconv = _last_rows(conv_state, main3[:, :, :CW])

            qn, qr, ckv, ckvb, kr, krb = _mla_proj(
                main, small, jnp.asarray(cos_t), jnp.asarray(sin_t), wq_all, g_q_a[l], g_kv_a[l],
                tl=tl, H=H, NOPE=NOPE, ROPE=ROPE, QL=QL, KVL=KVL, scale=scale, table_blocks=tblk)
            return dict(x=x, main=main, o_gdn=o_gdn, s_new=s_new, new_gconv=new_gconv, qn=qn, qr=qr,
                        ckv=ckv, ckvb=ckvb, kr=kr, krb=krb, stream_out=stream_out, nb=nb, L=L)

        def back(c, o_mla, p_l, *, prompt, stream):
            x, main, o_gdn, nb, L = c["x"], c["main"], c["o_gdn"], c["nb"], c["L"]
            T = x.shape[0]
            tr, tm = tiles(T)
            tn = TN
            stream_out = None
            merged = _matmul([(o_gdn, wo_gdn), (o_mla, wo_mla)],
                             [(main, (CW + H * DV) // tn), (main, (CW + H * DV + D) // tn)],
                             _epi_merge, BF16, tm, tn, "merge")
            x1 = _matmul([(merged, wout)], [(x, 0)], _epi_resid, F32, tm, tn, "out_proj")

            if prompt:
                fstate = jnp.zeros((nb, FK - 1, F2), F32)
                h2 = _rmsnorm(x1, g_ffn[l], BF16, tr)
                act, new_fconv, stream_out = _up_conv_act(h2, wup, ffn_conv_w[l], fstate, L=L, tm=tm,
                                                          tn=TN_UP, stream=stream)
            else:
                fstate = state_ffn_conv[l]
                up = _matmul([(x1, wup)], [], _epi_plain, F32, tm, tn, "up_proj",
                             prenorm=g_ffn[l])
                up3 = up.reshape(nb, L, F2)
                xt = jnp.concatenate([jnp.swapaxes(fstate, 0, 1), jnp.swapaxes(up3, 0, 1)], axis=0)
                act = _ffn_conv_timemajor(xt, ffn_conv_w[l], tc=512)
                act = jnp.swapaxes(act, 0, 1).reshape(T, F)
                new_fconv = _last_rows(fstate, up3)
            x2 = _matmul([(act, wdown)], [(x1, 0)], _epi_resid, F32, tr, 1024, "down_proj")

            last_layer = l == depth - 1
            x3 = _matmul([(x2, wpg), (p_l.reshape(T, -1).astype(BF16), wpp)], [],
                         _epi_ple_final if last_layer else _epi_ple, F32, min(256, T), D, "ple",
                         prenorm=g_ple[l], rowvecs=(g_final,) if last_layer else (),
                         a_as_extra=True)
            return x3, new_fconv, stream_out

        cs = front(xs, prompt=False, stream=None)
        ql3 = _head_matmul(cs["qn"], wuk_t, BF16, "q_lat").reshape(DB, TS * H, KVL)
        qr3 = cs["qr"].reshape(DB, TS * H, LANES)
        cn3 = cs["ckv"].reshape(DB, TS, KVL)
        krn3 = cs["krb"].astype(F32).reshape(DB, TS, LANES)

        TP = BP * LP
        tm_p = tiles(TP)[1]
        n_a = min(DB, (W_MAIN // TN) * (TP // tm_p))
        n_b = min(DB - n_a, (F // TN_UP) * (TP // tm_p))
        p_stream = NPG // 8 if NPG % 16 == 0 else 0

        def page_stream(lo, hi):
            if hi == lo or p_stream == 0:
                return None
            return _PageStream(page_table[lo:hi], ql3[lo:hi], qr3[lo:hi], cn3[lo:hi], krn3[lo:hi],
                               cache_ckv, cache_krt, P=p_stream, H=H, T=TS, layer=l)

        if p_stream == 0:
            n_a = n_b = 0

        cp = front(xp, prompt=True, stream=page_stream(0, n_a))
        o_mla_p = _prompt_attention(
            cp["qn"].reshape(BP, LP, H * NOPE), cp["qr"].reshape(BP, LP, H * LANES),
            cp["ckvb"].reshape(BP, LP, KVL), cp["krb"].reshape(BP, LP, LANES), wuk_t, wuv,
            tq=128, tk=512).reshape(TP, H * VH)
        xp, fconv_p, so_b = back(cp, o_mla_p, p_prompt[l], prompt=True,
                                 stream=page_stream(n_a, n_a + n_b))
        outs["ckv_p"].append(cp["ckv"].reshape(BP, LP // PS, PS, KVL))
        outs["kr_p"].append(cp["kr"].reshape(BP, LP // PS, PS, ROPE))
        outs["gdn_p"].append(cp["s_new"])
        outs["gc_p"].append(cp["new_gconv"])
        outs["fc_p"].append(fconv_p)

        parts = [o for o in (cp["stream_out"], so_b) if o is not None]
        n_c = n_a + n_b
        if n_c < DB:
            parts.append(_sample_attention(
                page_table[n_c:], ql3[n_c:], qr3[n_c:], cn3[n_c:], krn3[n_c:],
                cache_ckv, cache_krt, P=min(32, NPG), H=H, T=TS, layer=l))
        o_lat = parts[0] if len(parts) == 1 else jnp.concatenate(parts, axis=0)
        o_mla_s = _head_matmul(o_lat.reshape(DB * TS, H * KVL).astype(BF16), wuv, BF16, "o_mla")
        xs, fconv_s, _ = back(cs, o_mla_s, p_sample[l], prompt=False, stream=None)
        outs["ckv_s"].append(cs["ckv"].reshape(DB, TS, KVL))
        outs["kr_s"].append(cs["kr"].reshape(DB, TS, ROPE))
        outs["gdn_s"].append(cs["s_new"])
        outs["gc_s"].append(cs["new_gconv"])
        outs["fc_s"].append(fconv_s)

    y_prompt = xp.reshape(BP, LP, D)
    y_sample = xs.reshape(DB, TS, D)
    st = lambda k: jnp.stack(outs[k])
    return (y_prompt, y_sample, st("ckv_p"), st("kr_p"), st("ckv_s"), st("kr_s"),
            st("gdn_p"), st("gdn_s"), st("gc_p"), st("gc_s"), st("fc_p"), st("fc_s"))
```

```python
import functools
import math

import jax
import jax.numpy as jnp
import numpy as np
from jax import lax
from jax.experimental import pallas as pl
from jax.experimental.pallas import tpu as pltpu

F32 = jnp.float32
BF16 = jnp.bfloat16
EPS = 1e-6
ROPE_BASE = 10000.0
HP = lax.Precision.HIGHEST
LANES = 128
SUBLANES = 8
NEG = -1e30
VMEM_LIMIT = 56 * 1024 * 1024

NT = (((1,), (1,)), ((), ()))
TN = (((0,), (0,)), ((), ()))


def _cp(sem):
    return pltpu.CompilerParams(dimension_semantics=sem, vmem_limit_bytes=VMEM_LIMIT)


def _sigmoid(x):
    return 1.0 / (1.0 + jnp.exp(-x))


def _silu(x):
    return x * _sigmoid(x)


def _bdot(a, b, dims=None):
    a = a.astype(BF16)
    b = b.astype(BF16)
    if dims is None:
        return jnp.dot(a, b, preferred_element_type=F32)
    return lax.dot_general(a, b, dims, preferred_element_type=F32)


def _rms_body(x_ref, g_ref, o_ref):
    x = x_ref[...]
    ms = jnp.mean(x * x, axis=-1, keepdims=True)
    o_ref[...] = (x * lax.rsqrt(ms + EPS) * g_ref[...]).astype(o_ref.dtype)


def _rmsnorm(x, g, out_dtype, tm):
    M, D = x.shape
    return pl.pallas_call(
        _rms_body,
        grid=(M // tm,),
        in_specs=[pl.BlockSpec((tm, D), lambda i: (i, 0)),
                  pl.BlockSpec((1, D), lambda i: (0, 0))],
        out_specs=pl.BlockSpec((tm, D), lambda i: (i, 0)),
        out_shape=jax.ShapeDtypeStruct((M, D), out_dtype),
        compiler_params=_cp(("parallel",)),
        name="rmsnorm",
    )(x, g.reshape(1, D))


def _rms_rows(x, g):
    return x * lax.rsqrt(jnp.mean(x * x, axis=-1, keepdims=True) + EPS) * g


def _mm_body(*refs, n_pairs, n_extras, epi, prenorm, a_as_extra):
    o_ref = refs[-1]
    a_raw = refs[0][...]
    a0 = _rms_rows(a_raw, refs[-2][...]).astype(BF16) if prenorm else a_raw
    accs = [jnp.dot(a0, refs[1][...], preferred_element_type=F32)]
    accs += [jnp.dot(refs[2 * p][...].astype(BF16), refs[2 * p + 1][...],
                     preferred_element_type=F32) for p in range(1, n_pairs)]
    extras = [a_raw] if a_as_extra else []
    extras += [r[...] for r in refs[2 * n_pairs:2 * n_pairs + n_extras]]
    o_ref[...] = epi(accs, extras).astype(o_ref.dtype)


def _matmul(pairs, extras, epi, out_dtype, tm, tn, name, prenorm=None, rowvecs=(),
            a_as_extra=False):
    M = pairs[0][0].shape[0]
    N = pairs[0][1].shape[1]
    in_specs, args = [], []
    for a, w in pairs:
        K = a.shape[1]
        in_specs += [pl.BlockSpec((tm, K), lambda j, i: (i, 0)),
                     pl.BlockSpec((K, tn), lambda j, i: (0, j))]
        args += [a, w]
    for e, off in extras:
        in_specs.append(pl.BlockSpec((tm, tn), lambda j, i, off=off: (i, off + j)))
        args.append(e)
    for v in rowvecs:
        in_specs.append(pl.BlockSpec((1, tn), lambda j, i: (0, j)))
        args.append(v.reshape(1, N))
    if prenorm is not None:
        K0 = pairs[0][0].shape[1]
        in_specs.append(pl.BlockSpec((1, K0), lambda j, i: (0, 0)))
        args.append(prenorm.reshape(1, K0))
    return pl.pallas_call(
        functools.partial(_mm_body, n_pairs=len(pairs), n_extras=len(extras) + len(rowvecs),
                          epi=epi, prenorm=prenorm is not None, a_as_extra=a_as_extra),
        grid=(N // tn, M // tm),
        in_specs=in_specs,
        out_specs=pl.BlockSpec((tm, tn), lambda j, i: (i, j)),
        out_shape=jax.ShapeDtypeStruct((M, N), out_dtype),
        compiler_params=_cp(("parallel", "parallel")),
        name=name,
    )(*args)


def _epi_plain(accs, extras):
    return accs[0]


def _epi_merge(accs, extras):
    return _sigmoid(extras[0]) * accs[0] + _sigmoid(extras[1]) * accs[1]


def _epi_resid(accs, extras):
    return extras[0] + accs[0]


def _epi_ple(accs, extras):
    return extras[0] + _sigmoid(accs[0]) * accs[1]


def _epi_ple_final(accs, extras):
    return _rms_rows(extras[0] + _sigmoid(accs[0]) * accs[1], extras[1])


def _headmm_body(a_ref, w_ref, o_ref):
    o_ref[...] = jnp.dot(a_ref[...], w_ref[0], preferred_element_type=F32).astype(o_ref.dtype)


def _head_matmul(a, w, out_dtype, name):
    M = a.shape[0]
    H, K, N = w.shape
    return pl.pallas_call(
        _headmm_body,
        grid=(H,),
        in_specs=[pl.BlockSpec((M, K), lambda h: (0, h)),
                  pl.BlockSpec((1, K, N), lambda h: (h, 0, 0))],
        out_specs=pl.BlockSpec((M, N), lambda h: (0, h)),
        out_shape=jax.ShapeDtypeStruct((M, H * N), out_dtype),
        compiler_params=_cp(("parallel",)),
        name=name,
    )(a, w)


def _gdn_body(cin_ref, z_ref, sm_ref, cw_ref, cs_ref, s0_ref, al_ref, dtb_ref, nw_ref,
              o_ref, sout_ref, xp, S, smp, *, NB, C, Lb, H, DK, DV, KC):
    c = pl.program_id(1)
    last = pl.num_programs(1) - 1
    HIST = KC - 1
    R0 = SUBLANES

    @pl.when(c == 0)
    def _init():
        xp[...] = jnp.zeros_like(xp)
        for n in range(NB):
            xp[n, R0 - HIST:R0, :] = cs_ref[n]
        S[...] = s0_ref[...]
        smp[...] = jnp.zeros_like(smp)

    for n in range(NB):
        xp[n, R0:R0 + Lb, :] = cin_ref[n]
        smp[n, 0:Lb, :] = sm_ref[n]

    lane = lax.broadcasted_iota(jnp.int32, (C, LANES), 1)
    row = lax.broadcasted_iota(jnp.int32, (C, LANES), 0)
    valid = row < Lb
    ri = lax.broadcasted_iota(jnp.int32, (C, C), 0)
    ci = lax.broadcasted_iota(jnp.int32, (C, C), 1)
    tril = ri >= ci
    strict = ri > ci
    eye = (ri == ci).astype(F32)
    trilf = tril.astype(F32)
    sel = (lax.broadcasted_iota(jnp.int32, (SUBLANES, LANES), 1)
           == lax.broadcasted_iota(jnp.int32, (SUBLANES, LANES), 0) + H).astype(F32)

    gates = []
    for n in range(NB):
        smv = smp[n]
        beta = jnp.where(valid & (lane < H), _sigmoid(smv), 0.0)
        xa = smv + dtb_ref[...]
        softplus = jnp.maximum(xa, 0.0) + jnp.log(1.0 + jnp.exp(-jnp.abs(xa)))
        g = jnp.where(valid & (lane >= H) & (lane < 2 * H), -jnp.exp(al_ref[...]) * softplus, 0.0)
        gc = jnp.dot(trilf, g, precision=HP, preferred_element_type=F32)
        gr = lax.dot_general(sel, gc, NT, precision=HP, preferred_element_type=F32)
        gates.append((beta, gc, gr))

    def conv(n, off):
        acc = xp[n, R0 - HIST:R0 - HIST + C, off:off + LANES] * cw_ref[0:1, off:off + LANES]
        for j in range(1, KC):
            acc = acc + (xp[n, R0 - HIST + j:R0 - HIST + j + C, off:off + LANES]
                         * cw_ref[j:j + 1, off:off + LANES])
        return _silu(acc)

    chains = [(n, h) for n in range(NB) for h in range(H)]
    st = []
    for n, h in chains:
        beta, gc, gr = gates[n]
        q = conv(n, h * DK)
        k = conv(n, H * DK + h * DK)
        v = conv(n, 2 * H * DK + h * DV)
        q = q * lax.rsqrt(jnp.sum(q * q, axis=-1, keepdims=True) + EPS) * (DK ** -0.5)
        k = k * lax.rsqrt(jnp.sum(k * k, axis=-1, keepdims=True) + EPS)
        gcol = gc[:, H + h:H + h + 1]
        grow = gr[h:h + 1, :]
        bcol = beta[:, h:h + 1]
        glast = gc[C - 1:C, H + h:H + h + 1]
        decay = jnp.where(tril, jnp.exp(jnp.where(tril, gcol - grow, 0.0)), 0.0)
        kb = k * bcol
        egc = jnp.exp(gcol)
        st.append(dict(q=q, k=k, gcol=gcol, glast=glast, decay=decay, kb=kb,
                       rhs=jnp.concatenate([v * bcol, kb * egc], axis=1), qe=q * egc))

    for s in st:
        s["pw"] = -jnp.where(strict, _bdot(s["kb"], s["k"], NT) * s["decay"], 0.0)
        s["tinv"] = eye + s["pw"]
    n_levels = int(math.ceil(math.log2(C)))
    for s in st:
        s["pw"] = _bdot(s["pw"], s["pw"])
    for _ in range(n_levels - 2):
        for s in st:
            both = _bdot(jnp.concatenate([s["pw"], s["tinv"]], axis=0), s["pw"])
            s["pw"] = both[0:C]
            s["tinv"] = s["tinv"] + both[C:2 * C]
    for s in st:
        s["tinv"] = s["tinv"] + _bdot(s["tinv"], s["pw"])

    for s in st:
        sol = _bdot(s["tinv"], s["rhs"])
        s["u"] = sol[:, 0:DV]
        s["w"] = sol[:, DV:DV + DK]
        s["qk"] = _bdot(s["q"], s["k"], NT) * s["decay"]

    for (n, h), s in zip(chains, st):
        s["S"] = S[n, h]
        ws = _bdot(jnp.concatenate([s["w"], s["qe"]], axis=0), s["S"])
        s["v_new"] = s["u"] - ws[0:C]
        s["o"] = ws[C:2 * C]

    for (n, h), s in zip(chains, st):
        o = s["o"] + _bdot(s["qk"], s["v_new"])
        S[n, h] = (s["S"] * jnp.exp(s["glast"])
                   + _bdot(s["k"] * jnp.exp(s["glast"] - s["gcol"]), s["v_new"], TN))
        on = o * lax.rsqrt(jnp.mean(o * o, axis=-1, keepdims=True) + EPS) * nw_ref[...]
        zz = z_ref[n, :, h * DV:(h + 1) * DV]
        o_ref[n, :, h * DV:(h + 1) * DV] = (on[0:Lb] * _silu(zz)).astype(o_ref.dtype)

    if Lb == C:
        for n in range(NB):
            xp[n, R0 - HIST:R0, :] = xp[n, R0 + C - HIST:R0 + C, :]

    @pl.when(c == last)
    def _fin():
        sout_ref[...] = S[...]


def _gdn(main3, small3, conv_w, conv_state, s0, a_log, dt_bias, norm_w, *, NB, C, Lb, H, DK, DV,
         out_dtype):
    B, L, _ = main3.shape
    KC, CW = conv_w.shape
    nc = L // Lb
    assert nc == 1 or Lb == C
    assert B % NB == 0
    pad = lambda vec, off: jnp.zeros((1, LANES), F32).at[0, off:off + H].set(vec.astype(F32))
    body = functools.partial(_gdn_body, NB=NB, C=C, Lb=Lb, H=H, DK=DK, DV=DV, KC=KC)
    return pl.pallas_call(
        body,
        grid=(B // NB, nc),
        in_specs=[
            pl.BlockSpec((NB, Lb, CW), lambda b, c: (b, c, 0)),
            pl.BlockSpec((NB, Lb, H * DV), lambda b, c: (b, c, CW // (H * DV))),
            pl.BlockSpec((NB, Lb, LANES), lambda b, c: (b, c, 2)),
            pl.BlockSpec((KC, CW), lambda b, c: (0, 0)),
            pl.BlockSpec((NB, KC - 1, CW), lambda b, c: (b, 0, 0)),
            pl.BlockSpec((NB, H, DK, DV), lambda b, c: (b, 0, 0, 0)),
            pl.BlockSpec((1, LANES), lambda b, c: (0, 0)),
            pl.BlockSpec((1, LANES), lambda b, c: (0, 0)),
            pl.BlockSpec((1, DV), lambda b, c: (0, 0)),
        ],
        out_specs=[
            pl.BlockSpec((NB, Lb, H * DV), lambda b, c: (b, c, 0)),
            pl.BlockSpec((NB, H, DK, DV), lambda b, c: (b, 0, 0, 0)),
        ],
        out_shape=[jax.ShapeDtypeStruct((B, L, H * DV), out_dtype),
                   jax.ShapeDtypeStruct((B, H, DK, DV), F32)],
        scratch_shapes=[pltpu.VMEM((NB, SUBLANES + C, CW), F32),
                        pltpu.VMEM((NB, H, DK, DV), F32),
                        pltpu.VMEM((NB, C, LANES), F32)],
        compiler_params=_cp(("parallel", "arbitrary")),
        name="gdn",
    )(main3, main3, small3, conv_w, conv_state, s0, pad(a_log, H), pad(dt_bias, H),
      norm_w.reshape(1, DV).astype(F32))


def _mlaproj_body(cq_ref, ckv_ref, sm_ref, cos_ref, sin_ref, wq_ref, gq_ref, gkv_ref,
                  qn_ref, qr_ref, ckv_o_ref, ckvb_ref, kr_o_ref, krb_ref, *, H, NOPE, ROPE, scale):
    cq = cq_ref[...]
    cq = (cq * lax.rsqrt(jnp.mean(cq * cq, axis=-1, keepdims=True) + EPS) * gq_ref[...]).astype(BF16)
    q = jnp.dot(cq, wq_ref[...], preferred_element_type=F32)
    HW = H * LANES
    cosp = cos_ref[...]
    sinp = sin_ref[...]
    cos_h = jnp.concatenate([cosp] * H, axis=1)
    sin_h = jnp.concatenate([sinp] * H, axis=1)
    qn_ref[...] = (q[:, 0:H * NOPE] * scale).astype(qn_ref.dtype)
    qr = q[:, H * NOPE:H * NOPE + HW] * cos_h + q[:, H * NOPE + HW:H * NOPE + 2 * HW] * sin_h
    qr_ref[...] = (qr * scale).astype(qr_ref.dtype)
    ckv = ckv_ref[...]
    ckv = ckv * lax.rsqrt(jnp.mean(ckv * ckv, axis=-1, keepdims=True) + EPS) * gkv_ref[...]
    ckv_o_ref[...] = ckv
    ckvb_ref[...] = ckv.astype(BF16)
    sm = sm_ref[...]
    kr = sm[:, 0:LANES] * cosp + sm[:, LANES:2 * LANES] * sinp
    kr_o_ref[...] = kr[:, 0:ROPE]
    krb_ref[...] = kr.astype(BF16)


def _mla_proj(main2, small2, cosp, sinp, wq, g_q, g_kv, *, tl, H, NOPE, ROPE, QL, KVL, scale, table_blocks):
    T, W = main2.shape
    cq_blk = (W - QL - KVL) // QL
    assert QL == KVL
    nt = table_blocks
    body = functools.partial(_mlaproj_body, H=H, NOPE=NOPE, ROPE=ROPE, scale=scale)
    return pl.pallas_call(
        body,
        grid=(T // tl,),
        in_specs=[
            pl.BlockSpec((tl, QL), lambda i: (i, cq_blk)),
            pl.BlockSpec((tl, KVL), lambda i: (i, cq_blk + 1)),
            pl.BlockSpec((tl, 2 * LANES), lambda i: (i, 0)),
            pl.BlockSpec((tl, LANES), lambda i: (i % nt, 0)),
            pl.BlockSpec((tl, LANES), lambda i: (i % nt, 0)),
            pl.BlockSpec(wq.shape, lambda i: (0, 0)),
            pl.BlockSpec((1, QL), lambda i: (0, 0)),
            pl.BlockSpec((1, KVL), lambda i: (0, 0)),
        ],
        out_specs=[
            pl.BlockSpec((tl, H * NOPE), lambda i: (i, 0)),
            pl.BlockSpec((tl, H * LANES), lambda i: (i, 0)),
            pl.BlockSpec((tl, KVL), lambda i: (i, 0)),
            pl.BlockSpec((tl, KVL), lambda i: (i, 0)),
            pl.BlockSpec((tl, ROPE), lambda i: (i, 0)),
            pl.BlockSpec((tl, LANES), lambda i: (i, 0)),
        ],
        out_shape=[
            jax.ShapeDtypeStruct((T, H * NOPE), BF16),
            jax.ShapeDtypeStruct((T, H * LANES), BF16),
            jax.ShapeDtypeStruct((T, KVL), F32),
            jax.ShapeDtypeStruct((T, KVL), BF16),
            jax.ShapeDtypeStruct((T, ROPE), F32),
            jax.ShapeDtypeStruct((T, LANES), BF16),
        ],
        compiler_params=_cp(("parallel",)),
        name="mla_proj",
    )(main2, main2, small2, cosp, sinp, wq, g_q.reshape(1, QL), g_kv.reshape(1, KVL))


def _pattn_body(qn_ref, qr_ref, k_ref, kr_ref, wuk_ref, wuv_ref, o_ref,
                ql_s, qr_s, m_s, l_s, acc_s, *, H, tq, tk, NOPE, VH):
    i = pl.program_id(1)
    for h in range(H):
        ql_s[h * tq:(h + 1) * tq, :] = jnp.dot(
            qn_ref[0, :, h * NOPE:(h + 1) * NOPE], wuk_ref[h], preferred_element_type=F32).astype(BF16)
        qr_s[h * tq:(h + 1) * tq, :] = qr_ref[0, :, h * LANES:(h + 1) * LANES]
    m_s[...] = jnp.full_like(m_s, NEG)
    l_s[...] = jnp.zeros_like(l_s)
    acc_s[...] = jnp.zeros_like(acc_s)
    R = H * tq
    qpos = i * tq + (lax.broadcasted_iota(jnp.int32, (R, tk), 0) & (tq - 1))
    kofs = lax.broadcasted_iota(jnp.int32, (R, tk), 1)
    nkv = (i * tq + tq + tk - 1) // tk

    def scores(j):
        start = pl.multiple_of(j * tk, tk)
        return (lax.dot_general(ql_s[...], k_ref[0, pl.ds(start, tk), :], NT,
                                preferred_element_type=F32)
                + lax.dot_general(qr_s[...], kr_ref[0, pl.ds(start, tk), :], NT,
                                  preferred_element_type=F32))

    def absorb(s, j, masked):
        start = pl.multiple_of(j * tk, tk)
        if masked:
            s = jnp.where(kofs + j * tk <= qpos, s, NEG)
        m_prev = m_s[...]
        m_new = jnp.maximum(m_prev, jnp.max(s, axis=-1, keepdims=True))
        alpha = jnp.exp(m_prev - m_new)
        p = jnp.exp(s - m_new)
        l_s[...] = alpha * l_s[...] + jnp.sum(p, axis=-1, keepdims=True)
        acc_s[...] = alpha * acc_s[...] + jnp.dot(p.astype(BF16), k_ref[0, pl.ds(start, tk), :],
                                                  preferred_element_type=F32)
        m_s[...] = m_new

    def full_chunk(j, s_cur):
        s_next = scores(j + 1)
        absorb(s_cur, j, False)
        return s_next

    s_last = lax.fori_loop(0, nkv - 1, full_chunk, scores(0))
    absorb(s_last, nkv - 1, True)
    o = (acc_s[...] / l_s[...]).astype(BF16)
    for h in range(H):
        o_ref[0, :, h * VH:(h + 1) * VH] = jnp.dot(
            o[h * tq:(h + 1) * tq, :], wuv_ref[h], preferred_element_type=F32).astype(o_ref.dtype)


def _prompt_attention(qn, qr, ckvb, krb, wuk_t, wuv, *, tq, tk):
    B, L, _ = qn.shape
    H, NOPE, KVL = wuk_t.shape
    VH = wuv.shape[2]
    assert tq & (tq - 1) == 0 and tk % tq == 0 and H % 2 == 0
    body = functools.partial(_pattn_body, H=H, tq=tq, tk=tk, NOPE=NOPE, VH=VH)
    return pl.pallas_call(
        body,
        grid=(B, L // tq),
        in_specs=[
            pl.BlockSpec((1, tq, H * NOPE), lambda b, i: (b, i, 0)),
            pl.BlockSpec((1, tq, H * LANES), lambda b, i: (b, i, 0)),
            pl.BlockSpec((1, L, KVL), lambda b, i: (b, 0, 0)),
            pl.BlockSpec((1, L, LANES), lambda b, i: (b, 0, 0)),
            pl.BlockSpec((H, NOPE, KVL), lambda b, i: (0, 0, 0)),
            pl.BlockSpec((H, KVL, VH), lambda b, i: (0, 0, 0)),
        ],
        out_specs=pl.BlockSpec((1, tq, H * VH), lambda b, i: (b, i, 0)),
        out_shape=jax.ShapeDtypeStruct((B, L, H * VH), BF16),
        scratch_shapes=[pltpu.VMEM((H * tq, KVL), BF16),
                        pltpu.VMEM((H * tq, LANES), BF16),
                        pltpu.VMEM((H * tq, 1), F32),
                        pltpu.VMEM((H * tq, 1), F32),
                        pltpu.VMEM((H * tq, KVL), F32)],
        compiler_params=_cp(("parallel", "arbitrary")),
        name="prompt_attn",
    )(qn, qr, ckvb, krb, wuk_t, wuv)


def _sattn_body(pt_ref, ql_ref, qr_ref, cn_ref, krn_ref, cc_hbm, ckt_hbm, o_ref,
                kbuf, rbuf, sem, m_s, l_s, acc_s, knew, rnew, *, P, NC, PS, ROPE, H, T, layer):
    b = pl.program_id(0)
    c = pl.program_id(1)
    step = b * NC + c
    nsteps = pl.num_programs(0) * NC
    slot = step % 2

    def page_copies(stp, sl):
        bb = stp // NC
        cc = stp % NC
        cps = []
        for p in range(P):
            pg = pt_ref[bb, cc * P + p]
            cps.append(pltpu.make_async_copy(cc_hbm.at[layer, pg], kbuf.at[sl, p], sem.at[0, sl]))
            cps.append(pltpu.make_async_copy(ckt_hbm.at[layer, pg],
                                             rbuf.at[sl, :, pl.ds(p * PS, PS)], sem.at[1, sl]))
        return cps

    @pl.when(step == 0)
    def _first():
        for cp in page_copies(0, 0):
            cp.start()

    @pl.when(step + 1 < nsteps)
    def _prefetch():
        for cp in page_copies(step + 1, 1 - slot):
            cp.start()

    for cp in page_copies(step, slot):
        cp.wait()

    @pl.when(c == 0)
    def _init():
        m_s[...] = jnp.full_like(m_s, NEG)
        l_s[...] = jnp.zeros_like(l_s)
        acc_s[...] = jnp.zeros_like(acc_s)

    ql = ql_ref[0]
    qr = qr_ref[0]

    def update(s, vals):
        m_prev = m_s[...]
        m_new = jnp.maximum(m_prev, jnp.max(s, axis=-1, keepdims=True))
        alpha = jnp.exp(m_prev - m_new)
        p = jnp.exp(s - m_new)
        l_s[...] = alpha * l_s[...] + jnp.sum(p, axis=-1, keepdims=True)
        acc_s[...] = alpha * acc_s[...] + jnp.dot(p.astype(BF16), vals, preferred_element_type=F32)
        m_s[...] = m_new

    k = kbuf[slot].reshape(P * PS, kbuf.shape[-1]).astype(BF16)
    s_rope = jnp.dot(qr[:, 0:ROPE], rbuf[slot].astype(BF16), preferred_element_type=F32)
    s = lax.dot_general(ql, k, NT, preferred_element_type=F32) + s_rope
    update(s, k)

    @pl.when(c == NC - 1)
    def _fin():
        knew[...] = jnp.zeros_like(knew)
        rnew[...] = jnp.zeros_like(rnew)
        knew[0:T, :] = cn_ref[0]
        rnew[0:T, :] = krn_ref[0]
        kn = knew[...].astype(BF16)
        rn = rnew[...].astype(BF16)
        s2 = (lax.dot_general(ql, kn, NT, preferred_element_type=F32)
              + lax.dot_general(qr, rn, NT, preferred_element_type=F32))
        R = T * H
        trow = lax.broadcasted_iota(jnp.int32, (R, LANES), 0) // H
        kcol = lax.broadcasted_iota(jnp.int32, (R, LANES), 1)
        s2 = jnp.where(kcol <= trow, s2, NEG)
        update(s2, kn)
        o_ref[0] = acc_s[...] / l_s[...]


def _sample_attention(page_table, ql, qr, cn, krn, cache_ckv, cache_krt, *, P, H, T, layer):
    DB, NPG = page_table.shape
    _, _, PS, KVL = cache_ckv.shape
    ROPE = cache_krt.shape[2]
    NC = NPG // P
    R = T * H
    body = functools.partial(_sattn_body, P=P, NC=NC, PS=PS, ROPE=ROPE, H=H, T=T, layer=layer)
    grid_spec = pltpu.PrefetchScalarGridSpec(
        num_scalar_prefetch=1,
        grid=(DB, NC),
        in_specs=[
            pl.BlockSpec((1, R, KVL), lambda b, c, pt: (b, 0, 0)),
            pl.BlockSpec((1, R, LANES), lambda b, c, pt: (b, 0, 0)),
            pl.BlockSpec((1, T, KVL), lambda b, c, pt: (b, 0, 0)),
            pl.BlockSpec((1, T, LANES), lambda b, c, pt: (b, 0, 0)),
            pl.BlockSpec(memory_space=pl.ANY),
            pl.BlockSpec(memory_space=pl.ANY),
        ],
        out_specs=pl.BlockSpec((1, R, KVL), lambda b, c, pt: (b, 0, 0)),
        scratch_shapes=[pltpu.VMEM((2, P, PS, KVL), F32),
                        pltpu.VMEM((2, ROPE, P * PS), F32),
                        pltpu.SemaphoreType.DMA((2, 2)),
                        pltpu.VMEM((R, 1), F32),
                        pltpu.VMEM((R, 1), F32),
                        pltpu.VMEM((R, KVL), F32),
                        pltpu.VMEM((LANES, KVL), F32),
                        pltpu.VMEM((LANES, LANES), F32)],
    )
    return pl.pallas_call(
        body,
        grid_spec=grid_spec,
        out_shape=jax.ShapeDtypeStruct((DB, R, KVL), F32),
        compiler_params=_cp(("arbitrary", "arbitrary")),
        name="sample_attn",
    )(page_table, ql, qr, cn, krn, cache_ckv, cache_krt)


def _upconv_body(a_ref, wg_ref, wv_ref, cwg_ref, cwv_ref, sg_ref, sv_ref,
                 act_ref, ng_ref, nv_ref, xg, xv, *, tm, KC, tiles_per_seq):
    i = pl.program_id(1)
    HIST = KC - 1
    R0 = SUBLANES

    @pl.when(i % tiles_per_seq == 0)
    def _init():
        xg[R0 - HIST:R0, :] = sg_ref[0]
        xv[R0 - HIST:R0, :] = sv_ref[0]

    a = a_ref[...]
    xg[R0:R0 + tm, :] = jnp.dot(a, wg_ref[...], preferred_element_type=F32)
    xv[R0:R0 + tm, :] = jnp.dot(a, wv_ref[...], preferred_element_type=F32)

    def conv(x, w_ref):
        acc = x[R0 - HIST:R0 - HIST + tm, :] * w_ref[0:1, :]
        for j in range(1, KC):
            acc = acc + x[R0 - HIST + j:R0 - HIST + j + tm, :] * w_ref[j:j + 1, :]
        return acc

    act_ref[...] = (_silu(conv(xg, cwg_ref)) * conv(xv, cwv_ref)).astype(act_ref.dtype)
    tail_g = xg[R0 + tm - HIST:R0 + tm, :]
    tail_v = xv[R0 + tm - HIST:R0 + tm, :]
    ng_ref[0] = tail_g
    nv_ref[0] = tail_v
    xg[R0 - HIST:R0, :] = tail_g
    xv[R0 - HIST:R0, :] = tail_v


def _up_conv_act(h, w_up, conv_w, state, *, L, tm, tn):
    T, D = h.shape
    F2 = w_up.shape[1]
    F = F2 // 2
    B = T // L
    KC = conv_w.shape[0]
    nv = F // tn
    tps = L // tm
    body = functools.partial(_upconv_body, tm=tm, KC=KC, tiles_per_seq=tps)
    act, ng, nvv = pl.pallas_call(
        body,
        grid=(F // tn, T // tm),
        in_specs=[
            pl.BlockSpec((tm, D), lambda j, i: (i, 0)),
            pl.BlockSpec((D, tn), lambda j, i: (0, j)),
            pl.BlockSpec((D, tn), lambda j, i: (0, j + nv)),
            pl.BlockSpec((KC, tn), lambda j, i: (0, j)),
            pl.BlockSpec((KC, tn), lambda j, i: (0, j + nv)),
            pl.BlockSpec((1, KC - 1, tn), lambda j, i: (i // tps, 0, j)),
            pl.BlockSpec((1, KC - 1, tn), lambda j, i: (i // tps, 0, j + nv)),
        ],
        out_specs=[
            pl.BlockSpec((tm, tn), lambda j, i: (i, j)),
            pl.BlockSpec((1, KC - 1, tn), lambda j, i: (i // tps, 0, j)),
            pl.BlockSpec((1, KC - 1, tn), lambda j, i: (i // tps, 0, j)),
        ],
        out_shape=[jax.ShapeDtypeStruct((T, F), BF16),
                   jax.ShapeDtypeStruct((B, KC - 1, F), F32),
                   jax.ShapeDtypeStruct((B, KC - 1, F), F32)],
        scratch_shapes=[pltpu.VMEM((SUBLANES + tm, tn), F32), pltpu.VMEM((SUBLANES + tm, tn), F32)],
        compiler_params=_cp(("parallel", "arbitrary")),
        name="up_conv_act",
    )(h, w_up, w_up, conv_w, conv_w, state, state)
    return act, jnp.concatenate([ng, nvv], axis=2)


def _ffnconv_tm_body(xg_ref, xv_ref, wg_ref, wv_ref, o_ref, *, T, KC):
    for t in range(T):
        g = xg_ref[t] * wg_ref[0:1, :]
        v = xv_ref[t] * wv_ref[0:1, :]
        for j in range(1, KC):
            g = g + xg_ref[t + j] * wg_ref[j:j + 1, :]
            v = v + xv_ref[t + j] * wv_ref[j:j + 1, :]
        o_ref[t] = (_silu(g) * v).astype(o_ref.dtype)


def _ffn_conv_timemajor(xp, conv_w, *, tc):
    TT, NB, F2 = xp.shape
    F = F2 // 2
    KC = conv_w.shape[0]
    T = TT - (KC - 1)
    nv = F // tc
    body = functools.partial(_ffnconv_tm_body, T=T, KC=KC)
    return pl.pallas_call(
        body,
        grid=(F // tc,),
        in_specs=[
            pl.BlockSpec((TT, NB, tc), lambda j: (0, 0, j)),
            pl.BlockSpec((TT, NB, tc), lambda j: (0, 0, j + nv)),
            pl.BlockSpec((KC, tc), lambda j: (0, j)),
            pl.BlockSpec((KC, tc), lambda j: (0, j + nv)),
        ],
        out_specs=pl.BlockSpec((T, NB, tc), lambda j: (0, 0, j)),
        out_shape=jax.ShapeDtypeStruct((T, NB, F), BF16),
        compiler_params=_cp(("parallel",)),
        name="ffn_conv_tm",
    )(xp, xp, conv_w, conv_w)


def _prep_weights(w_in, w_uq, w_uk, w_uv, *, H, DK, DV, QL, KVL, NOPE, ROPE, VH, D):
    conv_ch = 2 * H * DK + H * DV
    sizes = (conv_ch, H * DV, H, H, QL, KVL, ROPE, D, D)
    offs = np.cumsum((0,) + sizes)
    seg = lambda n: w_in[:, offs[n]:offs[n + 1]]
    w_conv, w_z, w_b, w_a, w_cq, w_ckv, w_kr, w_ga, w_gb = (seg(n) for n in range(9))
    half = ROPE // 2
    w_main = jnp.concatenate([w_conv, w_z, w_ga, w_gb, w_cq, w_ckv], axis=1).astype(BF16)
    zc = lambda n: jnp.zeros((D, n), w_in.dtype)
    w_kr_sw = jnp.concatenate([w_kr[:, half:], w_kr[:, :half]], axis=1)
    w_small = jnp.concatenate([w_kr, zc(LANES - ROPE), w_kr_sw, zc(LANES - ROPE),
                               w_b, w_a, zc(LANES - 2 * H)], axis=1).astype(BF16)
    wq = w_uq.reshape(QL, H, NOPE + ROPE)
    wq_n = wq[:, :, :NOPE].reshape(QL, H * NOPE)
    wq_r = wq[:, :, NOPE:]
    zr = jnp.zeros((QL, H, LANES - ROPE), w_uq.dtype)
    wq_rp = jnp.concatenate([wq_r, zr], axis=2).reshape(QL, H * LANES)
    wq_rs = jnp.concatenate([wq_r[:, :, half:], wq_r[:, :, :half], zr], axis=2).reshape(QL, H * LANES)
    wq_all = jnp.concatenate([wq_n, wq_rp, wq_rs], axis=1).astype(BF16)
    wuk_t = jnp.transpose(w_uk, (1, 2, 0)).astype(BF16)
    wuv = jnp.transpose(w_uv, (1, 0, 2)).astype(BF16)
    return w_main, w_small, wq_all, wuk_t, wuv


def _rope_tables(pos, rope):
    half = rope // 2
    inv = np.exp(-math.log(ROPE_BASE) * np.arange(half, dtype=np.float64) / half)
    ang = np.asarray(pos).astype(np.float64)[:, None] * inv[None, :]
    c, s = np.cos(ang).astype(np.float32), np.sin(ang).astype(np.float32)
    z = np.zeros((ang.shape[0], LANES - rope), np.float32)
    return np.concatenate([c, c, z], axis=1), np.concatenate([-s, s, z], axis=1)


def _last_rows(hist, x):
    n, L = hist.shape[1], x.shape[1]
    if L >= n:
        return x[:, L - n:]
    return jnp.concatenate([hist[:, L:], x], axis=1)


def kernel(x_prompt, x_sample, cache_ckv, cache_krope, page_table, state_gdn, state_gdn_conv,
           state_ffn_conv, p_prompt, p_sample, g_mix, w_in, gdn_conv_w, gdn_A_log, gdn_dt_bias,
           gdn_norm_w, w_o_gdn, g_q_a, w_uq, g_kv_a, w_uk, w_uv, w_o_mla, w_out, g_ffn, w_up,
           ffn_conv_w, w_down, g_ple, w_ple_gate, w_ple_proj, g_final):
    BP, LP, D = x_prompt.shape
    DB, TS, _ = x_sample.shape
    depth = w_in.shape[0]
    H, DK, DV = state_gdn.shape[2], state_gdn.shape[3], state_gdn.shape[4]
    KC = gdn_conv_w.shape[1]
    CW = gdn_conv_w.shape[2]
    QL = g_q_a.shape[1]
    KVL, MH, NOPE = w_uk.shape[1], w_uk.shape[2], w_uk.shape[3]
    VH = w_uv.shape[3]
    ROPE = cache_krope.shape[3]
    PS = cache_ckv.shape[2]
    NPG = page_table.shape[1]
    F2 = w_up.shape[2]
    F = F2 // 2
    FK = ffn_conv_w.shape[1]
    assert H == MH
    scale = (NOPE + ROPE) ** -0.5
    past_len = NPG * PS
    W_MAIN = CW + H * DV + 2 * D + QL + KVL

    cos_p, sin_p = _rope_tables(np.arange(LP), ROPE)
    cos_s, sin_s = _rope_tables(past_len + np.arange(TS), ROPE)
    cos_s = np.tile(cos_s, (DB, 1))
    sin_s = np.tile(sin_s, (DB, 1))
    cache_krt = jnp.swapaxes(cache_krope, 2, 3)

    xp = x_prompt.reshape(BP * LP, D)
    xs = x_sample.reshape(DB * TS, D)
    outs = {k: [] for k in ("ckv_p", "kr_p", "ckv_s", "kr_s", "gdn_p", "gdn_s",
                            "gc_p", "gc_s", "fc_p", "fc_s")}

    for l in range(depth):
        w_main, w_small, wq_all, wuk_t, wuv = _prep_weights(
            w_in[l], w_uq[l], w_uk[l], w_uv[l], H=H, DK=DK, DV=DV, QL=QL, KVL=KVL,
            NOPE=NOPE, ROPE=ROPE, VH=VH, D=D)
        wo_gdn = w_o_gdn[l].astype(BF16)
        wo_mla = w_o_mla[l].astype(BF16)
        wout = w_out[l].astype(BF16)
        wup = w_up[l].astype(BF16)
        wdown = w_down[l].astype(BF16)
        wpg = w_ple_gate[l].astype(BF16)
        wpp = w_ple_proj[l].astype(BF16)

        def layer(x, p_l, *, prompt):
            T = x.shape[0]
            tr = min(512, T)
            tm = min(1024, T)
            tn = 1024
            h = _rmsnorm(x, g_mix[l], BF16, tr)
            main = _matmul([(h, w_main)], [], _epi_plain, F32, tm, tn, "in_proj")
            small = _matmul([(h, w_small)], [], _epi_plain, F32, tr, 3 * LANES, "in_proj_small")
            if prompt:
                nb, L = BP, LP
                conv_state = jnp.zeros((nb, KC - 1, CW), F32)
                s0 = jnp.zeros((nb, H, DK, DV), F32)
                cos_t, sin_t, chunk = cos_p, sin_p, 64
            else:
                nb, L = DB, TS
                conv_state = state_gdn_conv[l]
                s0 = state_gdn[l]
                cos_t, sin_t, chunk = cos_s, sin_s, SUBLANES
            tl = min(256, T)
            tblk = cos_t.shape[0] // tl
            main3 = main.reshape(nb, L, W_MAIN)
            small3 = small.reshape(nb, L, 3 * LANES)
            o_gdn, s_new = _gdn(main3, small3, gdn_conv_w[l], conv_state, s0, gdn_A_log[l],
                                gdn_dt_bias[l], gdn_norm_w[l], NB=2, C=chunk, Lb=min(chunk, L),
                                H=H, DK=DK, DV=DV, out_dtype=BF16 if prompt else F32)
            o_gdn = o_gdn.reshape(T, H * DV).astype(BF16)
            new_gconv = _last_rows(conv_state, main3[:, :, :CW])

            qn, qr, ckv, ckvb, kr, krb = _mla_proj(
                main, small, jnp.asarray(cos_t), jnp.asarray(sin_t), wq_all, g_q_a[l], g_kv_a[l],
                tl=tl, H=H, NOPE=NOPE, ROPE=ROPE, QL=QL, KVL=KVL, scale=scale, table_blocks=tblk)
            if prompt:
                o_mla = _prompt_attention(
                    qn.reshape(nb, L, H * NOPE), qr.reshape(nb, L, H * LANES),
                    ckvb.reshape(nb, L, KVL), krb.reshape(nb, L, LANES), wuk_t, wuv, tq=128, tk=512)
                o_mla = o_mla.reshape(T, H * VH)
            else:
                ql = _head_matmul(qn, wuk_t, BF16, "q_lat")
                o_lat = _sample_attention(
                    page_table, ql.reshape(nb, L * H, KVL), qr.reshape(nb, L * H, LANES),
                    ckv.reshape(nb, L, KVL), krb.astype(F32).reshape(nb, L, LANES),
                    cache_ckv, cache_krt, P=min(32, NPG), H=H, T=L, layer=l)
                o_mla = _head_matmul(o_lat.reshape(T, H * KVL).astype(BF16), wuv, BF16, "o_mla")

            merged = _matmul([(o_gdn, wo_gdn), (o_mla, wo_mla)],
                             [(main, (CW + H * DV) // tn), (main, (CW + H * DV + D) // tn)],
                             _epi_merge, BF16, tm, tn, "merge")
            x1 = _matmul([(merged, wout)], [(x, 0)], _epi_resid, F32, tm, tn, "out_proj")

            if prompt:
                fstate = jnp.zeros((nb, FK - 1, F2), F32)
                h2 = _rmsnorm(x1, g_ffn[l], BF16, tr)
                act, new_fconv = _up_conv_act(h2, wup, ffn_conv_w[l], fstate, L=L, tm=tm, tn=512)
            else:
                fstate = state_ffn_conv[l]
                up = _matmul([(x1, wup)], [], _epi_plain, F32, tm, tn, "up_proj",
                             prenorm=g_ffn[l])
                up3 = up.reshape(nb, L, F2)
                xt = jnp.concatenate([jnp.swapaxes(fstate, 0, 1), jnp.swapaxes(up3, 0, 1)], axis=0)
                act = _ffn_conv_timemajor(xt, ffn_conv_w[l], tc=512)
                act = jnp.swapaxes(act, 0, 1).reshape(T, F)
                new_fconv = _last_rows(fstate, up3)
            x2 = _matmul([(act, wdown)], [(x1, 0)], _epi_resid, F32, tr, 1024, "down_proj")

            last_layer = l == depth - 1
            x3 = _matmul([(x2, wpg), (p_l.reshape(T, -1), wpp)], [],
                         _epi_ple_final if last_layer else _epi_ple, F32, min(256, T), D, "ple",
                         prenorm=g_ple[l], rowvecs=(g_final,) if last_layer else (),
                         a_as_extra=True)
            return x3, (ckv, kr, s_new, new_gconv, new_fconv)

        xp, st = layer(xp, p_prompt[l], prompt=True)
        outs["ckv_p"].append(st[0].reshape(BP, LP // PS, PS, KVL))
        outs["kr_p"].append(st[1].reshape(BP, LP // PS, PS, ROPE))
        outs["gdn_p"].append(st[2])
        outs["gc_p"].append(st[3])
        outs["fc_p"].append(st[4])
        xs, st = layer(xs, p_sample[l], prompt=False)
        outs["ckv_s"].append(st[0].reshape(DB, TS, KVL))
        outs["kr_s"].append(st[1].reshape(DB, TS, ROPE))
        outs["gdn_s"].append(st[2])
        outs["gc_s"].append(st[3])
        outs["fc_s"].append(st[4])

    y_prompt = xp.reshape(BP, LP, D)
    y_sample = xs.reshape(DB, TS, D)
    st = lambda k: jnp.stack(outs[k])
    return (y_prompt, y_sample, st("ckv_p"), st("kr_p"), st("ckv_s"), st("kr_s"),
            st("gdn_p"), st("gdn_s"), st("gc_p"), st("gc_s"), st("fc_p"), st("fc_s"))
```

```python
import functools
import math

import jax
import jax.numpy as jnp
import numpy as np
from jax import lax
from jax.experimental import pallas as pl
from jax.experimental.pallas import tpu as pltpu

F32 = jnp.float32
BF16 = jnp.bfloat16
EPS = 1e-6
ROPE_BASE = 10000.0
HP = lax.Precision.HIGHEST
LANES = 128
SUBLANES = 8
NEG = -1e30
VMEM_LIMIT = 56 * 1024 * 1024

NT = (((1,), (1,)), ((), ()))
TN = (((0,), (0,)), ((), ()))


def _cp(sem):
    return pltpu.CompilerParams(dimension_semantics=sem, vmem_limit_bytes=VMEM_LIMIT)


def _sigmoid(x):
    return 1.0 / (1.0 + jnp.exp(-x))


def _silu(x):
    return x * _sigmoid(x)


def _bdot(a, b, dims=None):
    a = a.astype(BF16)
    b = b.astype(BF16)
    if dims is None:
        return jnp.dot(a, b, preferred_element_type=F32)
    return lax.dot_general(a, b, dims, preferred_element_type=F32)


def _rms_body(x_ref, g_ref, o_ref):
    x = x_ref[...]
    ms = jnp.mean(x * x, axis=-1, keepdims=True)
    o_ref[...] = (x * lax.rsqrt(ms + EPS) * g_ref[...]).astype(o_ref.dtype)


def _rmsnorm(x, g, out_dtype, tm):
    M, D = x.shape
    return pl.pallas_call(
        _rms_body,
        grid=(M // tm,),
        in_specs=[pl.BlockSpec((tm, D), lambda i: (i, 0)),
                  pl.BlockSpec((1, D), lambda i: (0, 0))],
        out_specs=pl.BlockSpec((tm, D), lambda i: (i, 0)),
        out_shape=jax.ShapeDtypeStruct((M, D), out_dtype),
        compiler_params=_cp(("parallel",)),
        name="rmsnorm",
    )(x, g.reshape(1, D))


def _rms_rows(x, g):
    return x * lax.rsqrt(jnp.mean(x * x, axis=-1, keepdims=True) + EPS) * g


def _mm_body(*refs, n_pairs, n_extras, epi, prenorm, a_as_extra):
    o_ref = refs[-1]
    a_raw = refs[0][...]
    a0 = _rms_rows(a_raw, refs[-2][...]).astype(BF16) if prenorm else a_raw
    accs = [jnp.dot(a0, refs[1][...], preferred_element_type=F32)]
    accs += [jnp.dot(refs[2 * p][...].astype(BF16), refs[2 * p + 1][...],
                     preferred_element_type=F32) for p in range(1, n_pairs)]
    extras = [a_raw] if a_as_extra else []
    extras += [r[...] for r in refs[2 * n_pairs:2 * n_pairs + n_extras]]
    o_ref[...] = epi(accs, extras).astype(o_ref.dtype)


def _matmul(pairs, extras, epi, out_dtype, tm, tn, name, prenorm=None, rowvecs=(),
            a_as_extra=False):
    M = pairs[0][0].shape[0]
    N = pairs[0][1].shape[1]
    in_specs, args = [], []
    for a, w in pairs:
        K = a.shape[1]
        in_specs += [pl.BlockSpec((tm, K), lambda j, i: (i, 0)),
                     pl.BlockSpec((K, tn), lambda j, i: (0, j))]
        args += [a, w]
    for e, off in extras:
        in_specs.append(pl.BlockSpec((tm, tn), lambda j, i, off=off: (i, off + j)))
        args.append(e)
    for v in rowvecs:
        in_specs.append(pl.BlockSpec((1, tn), lambda j, i: (0, j)))
        args.append(v.reshape(1, N))
    if prenorm is not None:
        K0 = pairs[0][0].shape[1]
        in_specs.append(pl.BlockSpec((1, K0), lambda j, i: (0, 0)))
        args.append(prenorm.reshape(1, K0))
    return pl.pallas_call(
        functools.partial(_mm_body, n_pairs=len(pairs), n_extras=len(extras) + len(rowvecs),
                          epi=epi, prenorm=prenorm is not None, a_as_extra=a_as_extra),
        grid=(N // tn, M // tm),
        in_specs=in_specs,
        out_specs=pl.BlockSpec((tm, tn), lambda j, i: (i, j)),
        out_shape=jax.ShapeDtypeStruct((M, N), out_dtype),
        compiler_params=_cp(("parallel", "parallel")),
        name=name,
    )(*args)


def _epi_plain(accs, extras):
    return accs[0]


def _epi_merge(accs, extras):
    return _sigmoid(extras[0]) * accs[0] + _sigmoid(extras[1]) * accs[1]


def _epi_resid(accs, extras):
    return extras[0] + accs[0]


def _epi_ple(accs, extras):
    return extras[0] + _sigmoid(accs[0]) * accs[1]


def _epi_ple_final(accs, extras):
    return _rms_rows(extras[0] + _sigmoid(accs[0]) * accs[1], extras[1])


def _headmm_body(a_ref, w_ref, o_ref):
    o_ref[...] = jnp.dot(a_ref[...], w_ref[0], preferred_element_type=F32).astype(o_ref.dtype)


def _head_matmul(a, w, out_dtype, name):
    M = a.shape[0]
    H, K, N = w.shape
    return pl.pallas_call(
        _headmm_body,
        grid=(H,),
        in_specs=[pl.BlockSpec((M, K), lambda h: (0, h)),
                  pl.BlockSpec((1, K, N), lambda h: (h, 0, 0))],
        out_specs=pl.BlockSpec((M, N), lambda h: (0, h)),
        out_shape=jax.ShapeDtypeStruct((M, H * N), out_dtype),
        compiler_params=_cp(("parallel",)),
        name=name,
    )(a, w)


def _gdn_body(cin_ref, z_ref, sm_ref, cw_ref, cs_ref, s0_ref, al_ref, dtb_ref, nw_ref,
              o_ref, sout_ref, xp, S, smp, *, NB, C, Lb, H, DK, DV, KC):
    c = pl.program_id(1)
    last = pl.num_programs(1) - 1
    HIST = KC - 1
    R0 = SUBLANES

    @pl.when(c == 0)
    def _init():
        xp[...] = jnp.zeros_like(xp)
        for n in range(NB):
            xp[n, R0 - HIST:R0, :] = cs_ref[n]
        S[...] = s0_ref[...]
        smp[...] = jnp.zeros_like(smp)

    for n in range(NB):
        xp[n, R0:R0 + Lb, :] = cin_ref[n]
        smp[n, 0:Lb, :] = sm_ref[n]

    lane = lax.broadcasted_iota(jnp.int32, (C, LANES), 1)
    row = lax.broadcasted_iota(jnp.int32, (C, LANES), 0)
    valid = row < Lb
    ri = lax.broadcasted_iota(jnp.int32, (C, C), 0)
    ci = lax.broadcasted_iota(jnp.int32, (C, C), 1)
    tril = ri >= ci
    strict = ri > ci
    eye = (ri == ci).astype(F32)
    trilf = tril.astype(F32)
    sel = (lax.broadcasted_iota(jnp.int32, (SUBLANES, LANES), 1)
           == lax.broadcasted_iota(jnp.int32, (SUBLANES, LANES), 0) + H).astype(F32)

    gates = []
    for n in range(NB):
        smv = smp[n]
        beta = jnp.where(valid & (lane < H), _sigmoid(smv), 0.0)
        xa = smv + dtb_ref[...]
        softplus = jnp.maximum(xa, 0.0) + jnp.log(1.0 + jnp.exp(-jnp.abs(xa)))
        g = jnp.where(valid & (lane >= H) & (lane < 2 * H), -jnp.exp(al_ref[...]) * softplus, 0.0)
        gc = jnp.dot(trilf, g, precision=HP, preferred_element_type=F32)
        gr = lax.dot_general(sel, gc, NT, precision=HP, preferred_element_type=F32)
        gates.append((beta, gc, gr))

    def conv(n, off):
        acc = xp[n, R0 - HIST:R0 - HIST + C, off:off + LANES] * cw_ref[0:1, off:off + LANES]
        for j in range(1, KC):
            acc = acc + (xp[n, R0 - HIST + j:R0 - HIST + j + C, off:off + LANES]
                         * cw_ref[j:j + 1, off:off + LANES])
        return _silu(acc)

    chains = [(n, h) for n in range(NB) for h in range(H)]
    st = []
    for n, h in chains:
        beta, gc, gr = gates[n]
        q = conv(n, h * DK)
        k = conv(n, H * DK + h * DK)
        v = conv(n, 2 * H * DK + h * DV)
        q = q * lax.rsqrt(jnp.sum(q * q, axis=-1, keepdims=True) + EPS) * (DK ** -0.5)
        k = k * lax.rsqrt(jnp.sum(k * k, axis=-1, keepdims=True) + EPS)
        gcol = gc[:, H + h:H + h + 1]
        grow = gr[h:h + 1, :]
        bcol = beta[:, h:h + 1]
        glast = gc[C - 1:C, H + h:H + h + 1]
        decay = jnp.where(tril, jnp.exp(jnp.where(tril, gcol - grow, 0.0)), 0.0)
        kb = k * bcol
        egc = jnp.exp(gcol)
        st.append(dict(q=q, k=k, gcol=gcol, glast=glast, decay=decay, kb=kb,
                       rhs=jnp.concatenate([v * bcol, kb * egc], axis=1), qe=q * egc))

    for s in st:
        s["pw"] = -jnp.where(strict, _bdot(s["kb"], s["k"], NT) * s["decay"], 0.0)
        s["tinv"] = eye + s["pw"]
    n_levels = int(math.ceil(math.log2(C)))
    for s in st:
        s["pw"] = _bdot(s["pw"], s["pw"])
    for _ in range(n_levels - 2):
        for s in st:
            both = _bdot(jnp.concatenate([s["pw"], s["tinv"]], axis=0), s["pw"])
            s["pw"] = both[0:C]
            s["tinv"] = s["tinv"] + both[C:2 * C]
    for s in st:
        s["tinv"] = s["tinv"] + _bdot(s["tinv"], s["pw"])

    for s in st:
        sol = _bdot(s["tinv"], s["rhs"])
        s["u"] = sol[:, 0:DV]
        s["w"] = sol[:, DV:DV + DK]
        s["qk"] = _bdot(s["q"], s["k"], NT) * s["decay"]

    for (n, h), s in zip(chains, st):
        s["S"] = S[n, h]
        ws = _bdot(jnp.concatenate([s["w"], s["qe"]], axis=0), s["S"])
        s["v_new"] = s["u"] - ws[0:C]
        s["o"] = ws[C:2 * C]

    for (n, h), s in zip(chains, st):
        o = s["o"] + _bdot(s["qk"], s["v_new"])
        S[n, h] = (s["S"] * jnp.exp(s["glast"])
                   + _bdot(s["k"] * jnp.exp(s["glast"] - s["gcol"]), s["v_new"], TN))
        on = o * lax.rsqrt(jnp.mean(o * o, axis=-1, keepdims=True) + EPS) * nw_ref[...]
        zz = z_ref[n, :, h * DV:(h + 1) * DV]
        o_ref[n, :, h * DV:(h + 1) * DV] = (on[0:Lb] * _silu(zz)).astype(o_ref.dtype)

    if Lb == C:
        for n in range(NB):
            xp[n, R0 - HIST:R0, :] = xp[n, R0 + C - HIST:R0 + C, :]

    @pl.when(c == last)
    def _fin():
        sout_ref[...] = S[...]


def _gdn(main3, small3, conv_w, conv_state, s0, a_log, dt_bias, norm_w, *, NB, C, Lb, H, DK, DV,
         out_dtype):
    B, L, _ = main3.shape
    KC, CW = conv_w.shape
    nc = L // Lb
    assert nc == 1 or Lb == C
    assert B % NB == 0
    pad = lambda vec, off: jnp.zeros((1, LANES), F32).at[0, off:off + H].set(vec.astype(F32))
    body = functools.partial(_gdn_body, NB=NB, C=C, Lb=Lb, H=H, DK=DK, DV=DV, KC=KC)
    return pl.pallas_call(
        body,
        grid=(B // NB, nc),
        in_specs=[
            pl.BlockSpec((NB, Lb, CW), lambda b, c: (b, c, 0)),
            pl.BlockSpec((NB, Lb, H * DV), lambda b, c: (b, c, CW // (H * DV))),
            pl.BlockSpec((NB, Lb, LANES), lambda b, c: (b, c, 2)),
            pl.BlockSpec((KC, CW), lambda b, c: (0, 0)),
            pl.BlockSpec((NB, KC - 1, CW), lambda b, c: (b, 0, 0)),
            pl.BlockSpec((NB, H, DK, DV), lambda b, c: (b, 0, 0, 0)),
            pl.BlockSpec((1, LANES), lambda b, c: (0, 0)),
            pl.BlockSpec((1, LANES), lambda b, c: (0, 0)),
            pl.BlockSpec((1, DV), lambda b, c: (0, 0)),
        ],
        out_specs=[
            pl.BlockSpec((NB, Lb, H * DV), lambda b, c: (b, c, 0)),
            pl.BlockSpec((NB, H, DK, DV), lambda b, c: (b, 0, 0, 0)),
        ],
        out_shape=[jax.ShapeDtypeStruct((B, L, H * DV), out_dtype),
                   jax.ShapeDtypeStruct((B, H, DK, DV), F32)],
        scratch_shapes=[pltpu.VMEM((NB, SUBLANES + C, CW), F32),
                        pltpu.VMEM((NB, H, DK, DV), F32),
                        pltpu.VMEM((NB, C, LANES), F32)],
        compiler_params=_cp(("parallel", "arbitrary")),
        name="gdn",
    )(main3, main3, small3, conv_w, conv_state, s0, pad(a_log, H), pad(dt_bias, H),
      norm_w.reshape(1, DV).astype(F32))


def _mlaproj_body(cq_ref, ckv_ref, sm_ref, cos_ref, sin_ref, wq_ref, gq_ref, gkv_ref,
                  qn_ref, qr_ref, ckv_o_ref, ckvb_ref, kr_o_ref, krb_ref, *, H, NOPE, ROPE, scale):
    cq = cq_ref[...]
    cq = (cq * lax.rsqrt(jnp.mean(cq * cq, axis=-1, keepdims=True) + EPS) * gq_ref[...]).astype(BF16)
    q = jnp.dot(cq, wq_ref[...], preferred_element_type=F32)
    HW = H * LANES
    cosp = cos_ref[...]
    sinp = sin_ref[...]
    cos_h = jnp.concatenate([cosp] * H, axis=1)
    sin_h = jnp.concatenate([sinp] * H, axis=1)
    qn_ref[...] = (q[:, 0:H * NOPE] * scale).astype(qn_ref.dtype)
    qr = q[:, H * NOPE:H * NOPE + HW] * cos_h + q[:, H * NOPE + HW:H * NOPE + 2 * HW] * sin_h
    qr_ref[...] = (qr * scale).astype(qr_ref.dtype)
    ckv = ckv_ref[...]
    ckv = ckv * lax.rsqrt(jnp.mean(ckv * ckv, axis=-1, keepdims=True) + EPS) * gkv_ref[...]
    ckv_o_ref[...] = ckv
    ckvb_ref[...] = ckv.astype(BF16)
    sm = sm_ref[...]
    kr = sm[:, 0:LANES] * cosp + sm[:, LANES:2 * LANES] * sinp
    kr_o_ref[...] = kr[:, 0:ROPE]
    krb_ref[...] = kr.astype(BF16)


def _mla_proj(main2, small2, cosp, sinp, wq, g_q, g_kv, *, tl, H, NOPE, ROPE, QL, KVL, scale, table_blocks):
    T, W = main2.shape
    cq_blk = (W - QL - KVL) // QL
    assert QL == KVL
    nt = table_blocks
    body = functools.partial(_mlaproj_body, H=H, NOPE=NOPE, ROPE=ROPE, scale=scale)
    return pl.pallas_call(
        body,
        grid=(T // tl,),
        in_specs=[
            pl.BlockSpec((tl, QL), lambda i: (i, cq_blk)),
            pl.BlockSpec((tl, KVL), lambda i: (i, cq_blk + 1)),
            pl.BlockSpec((tl, 2 * LANES), lambda i: (i, 0)),
            pl.BlockSpec((tl, LANES), lambda i: (i % nt, 0)),
            pl.BlockSpec((tl, LANES), lambda i: (i % nt, 0)),
            pl.BlockSpec(wq.shape, lambda i: (0, 0)),
            pl.BlockSpec((1, QL), lambda i: (0, 0)),
            pl.BlockSpec((1, KVL), lambda i: (0, 0)),
        ],
        out_specs=[
            pl.BlockSpec((tl, H * NOPE), lambda i: (i, 0)),
            pl.BlockSpec((tl, H * LANES), lambda i: (i, 0)),
            pl.BlockSpec((tl, KVL), lambda i: (i, 0)),
            pl.BlockSpec((tl, KVL), lambda i: (i, 0)),
            pl.BlockSpec((tl, ROPE), lambda i: (i, 0)),
            pl.BlockSpec((tl, LANES), lambda i: (i, 0)),
        ],
        out_shape=[
            jax.ShapeDtypeStruct((T, H * NOPE), BF16),
            jax.ShapeDtypeStruct((T, H * LANES), BF16),
            jax.ShapeDtypeStruct((T, KVL), F32),
            jax.ShapeDtypeStruct((T, KVL), BF16),
            jax.ShapeDtypeStruct((T, ROPE), F32),
            jax.ShapeDtypeStruct((T, LANES), BF16),
        ],
        compiler_params=_cp(("parallel",)),
        name="mla_proj",
    )(main2, main2, small2, cosp, sinp, wq, g_q.reshape(1, QL), g_kv.reshape(1, KVL))


def _pattn_body(qn_ref, qr_ref, k_ref, kr_ref, wuk_ref, wuv_ref, o_ref,
                ql_s, qr_s, m_s, l_s, acc_s, *, H, tq, tk, NOPE, VH):
    i = pl.program_id(1)
    for h in range(H):
        ql_s[h * tq:(h + 1) * tq, :] = jnp.dot(
            qn_ref[0, :, h * NOPE:(h + 1) * NOPE], wuk_ref[h], preferred_element_type=F32).astype(BF16)
        qr_s[h * tq:(h + 1) * tq, :] = qr_ref[0, :, h * LANES:(h + 1) * LANES]
    m_s[...] = jnp.full_like(m_s, NEG)
    l_s[...] = jnp.zeros_like(l_s)
    acc_s[...] = jnp.zeros_like(acc_s)
    R = H * tq
    qpos = i * tq + (lax.broadcasted_iota(jnp.int32, (R, tk), 0) & (tq - 1))
    kofs = lax.broadcasted_iota(jnp.int32, (R, tk), 1)
    nkv = (i * tq + tq + tk - 1) // tk

    def scores(j):
        start = pl.multiple_of(j * tk, tk)
        return (lax.dot_general(ql_s[...], k_ref[0, pl.ds(start, tk), :], NT,
                                preferred_element_type=F32)
                + lax.dot_general(qr_s[...], kr_ref[0, pl.ds(start, tk), :], NT,
                                  preferred_element_type=F32))

    def absorb(s, j, masked):
        start = pl.multiple_of(j * tk, tk)
        if masked:
            s = jnp.where(kofs + j * tk <= qpos, s, NEG)
        m_prev = m_s[...]
        m_new = jnp.maximum(m_prev, jnp.max(s, axis=-1, keepdims=True))
        alpha = jnp.exp(m_prev - m_new)
        p = jnp.exp(s - m_new)
        l_s[...] = alpha * l_s[...] + jnp.sum(p, axis=-1, keepdims=True)
        acc_s[...] = alpha * acc_s[...] + jnp.dot(p.astype(BF16), k_ref[0, pl.ds(start, tk), :],
                                                  preferred_element_type=F32)
        m_s[...] = m_new

    def full_chunk(j, s_cur):
        s_next = scores(j + 1)
        absorb(s_cur, j, False)
        return s_next

    s_last = lax.fori_loop(0, nkv - 1, full_chunk, scores(0))
    absorb(s_last, nkv - 1, True)
    o = (acc_s[...] / l_s[...]).astype(BF16)
    for h in range(H):
        o_ref[0, :, h * VH:(h + 1) * VH] = jnp.dot(
            o[h * tq:(h + 1) * tq, :], wuv_ref[h], preferred_element_type=F32).astype(o_ref.dtype)


def _prompt_attention(qn, qr, ckvb, krb, wuk_t, wuv, *, tq, tk):
    B, L, _ = qn.shape
    H, NOPE, KVL = wuk_t.shape
    VH = wuv.shape[2]
    assert tq & (tq - 1) == 0 and tk % tq == 0 and H % 2 == 0
    body = functools.partial(_pattn_body, H=H, tq=tq, tk=tk, NOPE=NOPE, VH=VH)
    return pl.pallas_call(
        body,
        grid=(B, L // tq),
        in_specs=[
            pl.BlockSpec((1, tq, H * NOPE), lambda b, i: (b, i, 0)),
            pl.BlockSpec((1, tq, H * LANES), lambda b, i: (b, i, 0)),
            pl.BlockSpec((1, L, KVL), lambda b, i: (b, 0, 0)),
            pl.BlockSpec((1, L, LANES), lambda b, i: (b, 0, 0)),
            pl.BlockSpec((H, NOPE, KVL), lambda b, i: (0, 0, 0)),
            pl.BlockSpec((H, KVL, VH), lambda b, i: (0, 0, 0)),
        ],
        out_specs=pl.BlockSpec((1, tq, H * VH), lambda b, i: (b, i, 0)),
        out_shape=jax.ShapeDtypeStruct((B, L, H * VH), BF16),
        scratch_shapes=[pltpu.VMEM((H * tq, KVL), BF16),
                        pltpu.VMEM((H * tq, LANES), BF16),
                        pltpu.VMEM((H * tq, 1), F32),
                        pltpu.VMEM((H * tq, 1), F32),
                        pltpu.VMEM((H * tq, KVL), F32)],
        compiler_params=_cp(("parallel", "arbitrary")),
        name="prompt_attn",
    )(qn, qr, ckvb, krb, wuk_t, wuv)


def _sattn_body(pt_ref, ql_ref, qr_ref, cn_ref, krn_ref, cc_hbm, ckt_hbm, o_ref,
                kbuf, rbuf, sem, m_s, l_s, acc_s, knew, rnew, *, P, NC, PS, ROPE, H, T, layer):
    b = pl.program_id(0)
    c = pl.program_id(1)
    step = b * NC + c
    nsteps = pl.num_programs(0) * NC
    slot = step % 2

    def page_copies(stp, sl):
        bb = stp // NC
        cc = stp % NC
        cps = []
        for p in range(P):
            pg = pt_ref[bb, cc * P + p]
            cps.append(pltpu.make_async_copy(cc_hbm.at[layer, pg], kbuf.at[sl, p], sem.at[0, sl]))
            cps.append(pltpu.make_async_copy(ckt_hbm.at[layer, pg],
                                             rbuf.at[sl, :, pl.ds(p * PS, PS)], sem.at[1, sl]))
        return cps

    @pl.when(step == 0)
    def _first():
        for n, cp in enumerate(page_copies(0, 0)):
            cp.start(priority=(n // 2) % 2)

    @pl.when(step + 1 < nsteps)
    def _prefetch():
        for n, cp in enumerate(page_copies(step + 1, 1 - slot)):
            cp.start(priority=(n // 2) % 2)

    for cp in page_copies(step, slot):
        cp.wait()

    @pl.when(c == 0)
    def _init():
        m_s[...] = jnp.full_like(m_s, NEG)
        l_s[...] = jnp.zeros_like(l_s)
        acc_s[...] = jnp.zeros_like(acc_s)

    ql = ql_ref[0]
    qr = qr_ref[0]

    def update(s, vals):
        m_prev = m_s[...]
        m_new = jnp.maximum(m_prev, jnp.max(s, axis=-1, keepdims=True))
        alpha = jnp.exp(m_prev - m_new)
        p = jnp.exp(s - m_new)
        l_s[...] = alpha * l_s[...] + jnp.sum(p, axis=-1, keepdims=True)
        acc_s[...] = alpha * acc_s[...] + jnp.dot(p.astype(BF16), vals, preferred_element_type=F32)
        m_s[...] = m_new

    k = kbuf[slot].reshape(P * PS, kbuf.shape[-1]).astype(BF16)
    s_rope = jnp.dot(qr[:, 0:ROPE], rbuf[slot].astype(BF16), preferred_element_type=F32)
    s = lax.dot_general(ql, k, NT, preferred_element_type=F32) + s_rope
    update(s, k)

    @pl.when(c == NC - 1)
    def _fin():
        knew[...] = jnp.zeros_like(knew)
        rnew[...] = jnp.zeros_like(rnew)
        knew[0:T, :] = cn_ref[0]
        rnew[0:T, :] = krn_ref[0]
        kn = knew[...].astype(BF16)
        rn = rnew[...].astype(BF16)
        s2 = (lax.dot_general(ql, kn, NT, preferred_element_type=F32)
              + lax.dot_general(qr, rn, NT, preferred_element_type=F32))
        R = T * H
        trow = lax.broadcasted_iota(jnp.int32, (R, LANES), 0) // H
        kcol = lax.broadcasted_iota(jnp.int32, (R, LANES), 1)
        s2 = jnp.where(kcol <= trow, s2, NEG)
        update(s2, kn)
        o_ref[0] = acc_s[...] / l_s[...]


def _sample_attention(page_table, ql, qr, cn, krn, cache_ckv, cache_krt, *, P, H, T, layer):
    DB, NPG = page_table.shape
    _, _, PS, KVL = cache_ckv.shape
    ROPE = cache_krt.shape[2]
    NC = NPG // P
    R = T * H
    body = functools.partial(_sattn_body, P=P, NC=NC, PS=PS, ROPE=ROPE, H=H, T=T, layer=layer)
    grid_spec = pltpu.PrefetchScalarGridSpec(
        num_scalar_prefetch=1,
        grid=(DB, NC),
        in_specs=[
            pl.BlockSpec((1, R, KVL), lambda b, c, pt: (b, 0, 0)),
            pl.BlockSpec((1, R, LANES), lambda b, c, pt: (b, 0, 0)),
            pl.BlockSpec((1, T, KVL), lambda b, c, pt: (b, 0, 0)),
            pl.BlockSpec((1, T, LANES), lambda b, c, pt: (b, 0, 0)),
            pl.BlockSpec(memory_space=pl.ANY),
            pl.BlockSpec(memory_space=pl.ANY),
        ],
        out_specs=pl.BlockSpec((1, R, KVL), lambda b, c, pt: (b, 0, 0)),
        scratch_shapes=[pltpu.VMEM((2, P, PS, KVL), F32),
                        pltpu.VMEM((2, ROPE, P * PS), F32),
                        pltpu.SemaphoreType.DMA((2, 2)),
                        pltpu.VMEM((R, 1), F32),
                        pltpu.VMEM((R, 1), F32),
                        pltpu.VMEM((R, KVL), F32),
                        pltpu.VMEM((LANES, KVL), F32),
                        pltpu.VMEM((LANES, LANES), F32)],
    )
    return pl.pallas_call(
        body,
        grid_spec=grid_spec,
        out_shape=jax.ShapeDtypeStruct((DB, R, KVL), F32),
        compiler_params=_cp(("arbitrary", "arbitrary")),
        name="sample_attn",
    )(page_table, ql, qr, cn, krn, cache_ckv, cache_krt)


def _upconv_body(a_ref, wg_ref, wv_ref, cwg_ref, cwv_ref, sg_ref, sv_ref,
                 act_ref, ng_ref, nv_ref, xg, xv, *, tm, KC, tiles_per_seq):
    i = pl.program_id(1)
    HIST = KC - 1
    R0 = SUBLANES

    @pl.when(i % tiles_per_seq == 0)
    def _init():
        xg[R0 - HIST:R0, :] = sg_ref[0]
        xv[R0 - HIST:R0, :] = sv_ref[0]

    a = a_ref[...]
    xg[R0:R0 + tm, :] = jnp.dot(a, wg_ref[...], preferred_element_type=F32)
    xv[R0:R0 + tm, :] = jnp.dot(a, wv_ref[...], preferred_element_type=F32)

    def conv(x, w_ref):
        acc = x[R0 - HIST:R0 - HIST + tm, :] * w_ref[0:1, :]
        for j in range(1, KC):
            acc = acc + x[R0 - HIST + j:R0 - HIST + j + tm, :] * w_ref[j:j + 1, :]
        return acc

    act_ref[...] = (_silu(conv(xg, cwg_ref)) * conv(xv, cwv_ref)).astype(act_ref.dtype)
    tail_g = xg[R0 + tm - HIST:R0 + tm, :]
    tail_v = xv[R0 + tm - HIST:R0 + tm, :]
    ng_ref[0] = tail_g
    nv_ref[0] = tail_v
    xg[R0 - HIST:R0, :] = tail_g
    xv[R0 - HIST:R0, :] = tail_v


def _up_conv_act(h, w_up, conv_w, state, *, L, tm, tn):
    T, D = h.shape
    F2 = w_up.shape[1]
    F = F2 // 2
    B = T // L
    KC = conv_w.shape[0]
    nv = F // tn
    tps = L // tm
    body = functools.partial(_upconv_body, tm=tm, KC=KC, tiles_per_seq=tps)
    act, ng, nvv = pl.pallas_call(
        body,
        grid=(F // tn, T // tm),
        in_specs=[
            pl.BlockSpec((tm, D), lambda j, i: (i, 0)),
            pl.BlockSpec((D, tn), lambda j, i: (0, j)),
            pl.BlockSpec((D, tn), lambda j, i: (0, j + nv)),
            pl.BlockSpec((KC, tn), lambda j, i: (0, j)),
            pl.BlockSpec((KC, tn), lambda j, i: (0, j + nv)),
            pl.BlockSpec((1, KC - 1, tn), lambda j, i: (i // tps, 0, j)),
            pl.BlockSpec((1, KC - 1, tn), lambda j, i: (i // tps, 0, j + nv)),
        ],
        out_specs=[
            pl.BlockSpec((tm, tn), lambda j, i: (i, j)),
            pl.BlockSpec((1, KC - 1, tn), lambda j, i: (i // tps, 0, j)),
            pl.BlockSpec((1, KC - 1, tn), lambda j, i: (i // tps, 0, j)),
        ],
        out_shape=[jax.ShapeDtypeStruct((T, F), BF16),
                   jax.ShapeDtypeStruct((B, KC - 1, F), F32),
                   jax.ShapeDtypeStruct((B, KC - 1, F), F32)],
        scratch_shapes=[pltpu.VMEM((SUBLANES + tm, tn), F32), pltpu.VMEM((SUBLANES + tm, tn), F32)],
        compiler_params=_cp(("parallel", "arbitrary")),
        name="up_conv_act",
    )(h, w_up, w_up, conv_w, conv_w, state, state)
    return act, jnp.concatenate([ng, nvv], axis=2)


def _ffnconv_tm_body(xg_ref, xv_ref, wg_ref, wv_ref, o_ref, *, T, KC):
    for t in range(T):
        g = xg_ref[t] * wg_ref[0:1, :]
        v = xv_ref[t] * wv_ref[0:1, :]
        for j in range(1, KC):
            g = g + xg_ref[t + j] * wg_ref[j:j + 1, :]
            v = v + xv_ref[t + j] * wv_ref[j:j + 1, :]
        o_ref[t] = (_silu(g) * v).astype(o_ref.dtype)


def _ffn_conv_timemajor(xp, conv_w, *, tc):
    TT, NB, F2 = xp.shape
    F = F2 // 2
    KC = conv_w.shape[0]
    T = TT - (KC - 1)
    nv = F // tc
    body = functools.partial(_ffnconv_tm_body, T=T, KC=KC)
    return pl.pallas_call(
        body,
        grid=(F // tc,),
        in_specs=[
            pl.BlockSpec((TT, NB, tc), lambda j: (0, 0, j)),
            pl.BlockSpec((TT, NB, tc), lambda j: (0, 0, j + nv)),
            pl.BlockSpec((KC, tc), lambda j: (0, j)),
            pl.BlockSpec((KC, tc), lambda j: (0, j + nv)),
        ],
        out_specs=pl.BlockSpec((T, NB, tc), lambda j: (0, 0, j)),
        out_shape=jax.ShapeDtypeStruct((T, NB, F), BF16),
        compiler_params=_cp(("parallel",)),
        name="ffn_conv_tm",
    )(xp, xp, conv_w, conv_w)


def _prep_weights(w_in, w_uq, w_uk, w_uv, *, H, DK, DV, QL, KVL, NOPE, ROPE, VH, D):
    conv_ch = 2 * H * DK + H * DV
    sizes = (conv_ch, H * DV, H, H, QL, KVL, ROPE, D, D)
    offs = np.cumsum((0,) + sizes)
    seg = lambda n: w_in[:, offs[n]:offs[n + 1]]
    w_conv, w_z, w_b, w_a, w_cq, w_ckv, w_kr, w_ga, w_gb = (seg(n) for n in range(9))
    half = ROPE // 2
    w_main = jnp.concatenate([w_conv, w_z, w_ga, w_gb, w_cq, w_ckv], axis=1).astype(BF16)
    zc = lambda n: jnp.zeros((D, n), w_in.dtype)
    w_kr_sw = jnp.concatenate([w_kr[:, half:], w_kr[:, :half]], axis=1)
    w_small = jnp.concatenate([w_kr, zc(LANES - ROPE), w_kr_sw, zc(LANES - ROPE),
                               w_b, w_a, zc(LANES - 2 * H)], axis=1).astype(BF16)
    wq = w_uq.reshape(QL, H, NOPE + ROPE)
    wq_n = wq[:, :, :NOPE].reshape(QL, H * NOPE)
    wq_r = wq[:, :, NOPE:]
    zr = jnp.zeros((QL, H, LANES - ROPE), w_uq.dtype)
    wq_rp = jnp.concatenate([wq_r, zr], axis=2).reshape(QL, H * LANES)
    wq_rs = jnp.concatenate([wq_r[:, :, half:], wq_r[:, :, :half], zr], axis=2).reshape(QL, H * LANES)
    wq_all = jnp.concatenate([wq_n, wq_rp, wq_rs], axis=1).astype(BF16)
    wuk_t = jnp.transpose(w_uk, (1, 2, 0)).astype(BF16)
    wuv = jnp.transpose(w_uv, (1, 0, 2)).astype(BF16)
    return w_main, w_small, wq_all, wuk_t, wuv


def _rope_tables(pos, rope):
    half = rope // 2
    inv = np.exp(-math.log(ROPE_BASE) * np.arange(half, dtype=np.float64) / half)
    ang = np.asarray(pos).astype(np.float64)[:, None] * inv[None, :]
    c, s = np.cos(ang).astype(np.float32), np.sin(ang).astype(np.float32)
    z = np.zeros((ang.shape[0], LANES - rope), np.float32)
    return np.concatenate([c, c, z], axis=1), np.concatenate([-s, s, z], axis=1)


def _last_rows(hist, x):
    n, L = hist.shape[1], x.shape[1]
    if L >= n:
        return x[:, L - n:]
    return jnp.concatenate([hist[:, L:], x], axis=1)


def kernel(x_prompt, x_sample, cache_ckv, cache_krope, page_table, state_gdn, state_gdn_conv,
           state_ffn_conv, p_prompt, p_sample, g_mix, w_in, gdn_conv_w, gdn_A_log, gdn_dt_bias,
           gdn_norm_w, w_o_gdn, g_q_a, w_uq, g_kv_a, w_uk, w_uv, w_o_mla, w_out, g_ffn, w_up,
           ffn_conv_w, w_down, g_ple, w_ple_gate, w_ple_proj, g_final):
    BP, LP, D = x_prompt.shape
    DB, TS, _ = x_sample.shape
    depth = w_in.shape[0]
    H, DK, DV = state_gdn.shape[2], state_gdn.shape[3], state_gdn.shape[4]
    KC = gdn_conv_w.shape[1]
    CW = gdn_conv_w.shape[2]
    QL = g_q_a.shape[1]
    KVL, MH, NOPE = w_uk.shape[1], w_uk.shape[2], w_uk.shape[3]
    VH = w_uv.shape[3]
    ROPE = cache_krope.shape[3]
    PS = cache_ckv.shape[2]
    NPG = page_table.shape[1]
    F2 = w_up.shape[2]
    F = F2 // 2
    FK = ffn_conv_w.shape[1]
    assert H == MH
    scale = (NOPE + ROPE) ** -0.5
    past_len = NPG * PS
    W_MAIN = CW + H * DV + 2 * D + QL + KVL

    cos_p, sin_p = _rope_tables(np.arange(LP), ROPE)
    cos_s, sin_s = _rope_tables(past_len + np.arange(TS), ROPE)
    cos_s = np.tile(cos_s, (DB, 1))
    sin_s = np.tile(sin_s, (DB, 1))
    cache_krt = jnp.swapaxes(cache_krope, 2, 3)

    xp = x_prompt.reshape(BP * LP, D)
    xs = x_sample.reshape(DB * TS, D)
    outs = {k: [] for k in ("ckv_p", "kr_p", "ckv_s", "kr_s", "gdn_p", "gdn_s",
                            "gc_p", "gc_s", "fc_p", "fc_s")}

    for l in range(depth):
        w_main, w_small, wq_all, wuk_t, wuv = _prep_weights(
            w_in[l], w_uq[l], w_uk[l], w_uv[l], H=H, DK=DK, DV=DV, QL=QL, KVL=KVL,
            NOPE=NOPE, ROPE=ROPE, VH=VH, D=D)
        wo_gdn = w_o_gdn[l].astype(BF16)
        wo_mla = w_o_mla[l].astype(BF16)
        wout = w_out[l].astype(BF16)
        wup = w_up[l].astype(BF16)
        wdown = w_down[l].astype(BF16)
        wpg = w_ple_gate[l].astype(BF16)
        wpp = w_ple_proj[l].astype(BF16)

        def layer(x, p_l, *, prompt):
            T = x.shape[0]
            tr = min(512, T)
            tm = min(1024, T)
            tn = 1024
            h = _rmsnorm(x, g_mix[l], BF16, tr)
            main = _matmul([(h, w_main)], [], _epi_plain, F32, tm, tn, "in_proj")
            small = _matmul([(h, w_small)], [], _epi_plain, F32, tr, 3 * LANES, "in_proj_small")
            if prompt:
                nb, L = BP, LP
                conv_state = jnp.zeros((nb, KC - 1, CW), F32)
                s0 = jnp.zeros((nb, H, DK, DV), F32)
                cos_t, sin_t, chunk = cos_p, sin_p, 64
            else:
                nb, L = DB, TS
                conv_state = state_gdn_conv[l]
                s0 = state_gdn[l]
                cos_t, sin_t, chunk = cos_s, sin_s, SUBLANES
            tl = min(256, T)
            tblk = cos_t.shape[0] // tl
            main3 = main.reshape(nb, L, W_MAIN)
            small3 = small.reshape(nb, L, 3 * LANES)
            o_gdn, s_new = _gdn(main3, small3, gdn_conv_w[l], conv_state, s0, gdn_A_log[l],
                                gdn_dt_bias[l], gdn_norm_w[l], NB=2, C=chunk, Lb=min(chunk, L),
                                H=H, DK=DK, DV=DV, out_dtype=BF16 if prompt else F32)
            o_gdn = o_gdn.reshape(T, H * DV).astype(BF16)
            new_gconv = _last_rows(conv_state, main3[:, :, :CW])

            qn, qr, ckv, ckvb, kr, krb = _mla_proj(
                main, small, jnp.asarray(cos_t), jnp.asarray(sin_t), wq_all, g_q_a[l], g_kv_a[l],
                tl=tl, H=H, NOPE=NOPE, ROPE=ROPE, QL=QL, KVL=KVL, scale=scale, table_blocks=tblk)
            if prompt:
                o_mla = _prompt_attention(
                    qn.reshape(nb, L, H * NOPE), qr.reshape(nb, L, H * LANES),
                    ckvb.reshape(nb, L, KVL), krb.reshape(nb, L, LANES), wuk_t, wuv, tq=128, tk=512)
                o_mla = o_mla.reshape(T, H * VH)
            else:
                ql = _head_matmul(qn, wuk_t, BF16, "q_lat")
                o_lat = _sample_attention(
                    page_table, ql.reshape(nb, L * H, KVL), qr.reshape(nb, L * H, LANES),
                    ckv.reshape(nb, L, KVL), krb.astype(F32).reshape(nb, L, LANES),
                    cache_ckv, cache_krt, P=min(32, NPG), H=H, T=L, layer=l)
                o_mla = _head_matmul(o_lat.reshape(T, H * KVL).astype(BF16), wuv, BF16, "o_mla")

            merged = _matmul([(o_gdn, wo_gdn), (o_mla, wo_mla)],
                             [(main, (CW + H * DV) // tn), (main, (CW + H * DV + D) // tn)],
                             _epi_merge, BF16, tm, tn, "merge")
            x1 = _matmul([(merged, wout)], [(x, 0)], _epi_resid, F32, tm, tn, "out_proj")

            if prompt:
                fstate = jnp.zeros((nb, FK - 1, F2), F32)
                h2 = _rmsnorm(x1, g_ffn[l], BF16, tr)
                act, new_fconv = _up_conv_act(h2, wup, ffn_conv_w[l], fstate, L=L, tm=tm, tn=512)
            else:
                fstate = state_ffn_conv[l]
                up = _matmul([(x1, wup)], [], _epi_plain, F32, tm, tn, "up_proj",
                             prenorm=g_ffn[l])
                up3 = up.reshape(nb, L, F2)
                xt = jnp.concatenate([jnp.swapaxes(fstate, 0, 1), jnp.swapaxes(up3, 0, 1)], axis=0)
                act = _ffn_conv_timemajor(xt, ffn_conv_w[l], tc=512)
                act = jnp.swapaxes(act, 0, 1).reshape(T, F)
                new_fconv = _last_rows(fstate, up3)
            x2 = _matmul([(act, wdown)], [(x1, 0)], _epi_resid, F32, tr, 1024, "down_proj")

            last_layer = l == depth - 1
            x3 = _matmul([(x2, wpg), (p_l.reshape(T, -1), wpp)], [],
                         _epi_ple_final if last_layer else _epi_ple, F32, min(256, T), D, "ple",
                         prenorm=g_ple[l], rowvecs=(g_final,) if last_layer else (),
                         a_as_extra=True)
            return x3, (ckv, kr, s_new, new_gconv, new_fconv)

        xp, st = layer(xp, p_prompt[l], prompt=True)
        outs["ckv_p"].append(st[0].reshape(BP, LP // PS, PS, KVL))
        outs["kr_p"].append(st[1].reshape(BP, LP // PS, PS, ROPE))
        outs["gdn_p"].append(st[2])
        outs["gc_p"].append(st[3])
        outs["fc_p"].append(st[4])
        xs, st = layer(xs, p_sample[l], prompt=False)
        outs["ckv_s"].append(st[0].reshape(DB, TS, KVL))
        outs["kr_s"].append(st[1].reshape(DB, TS, ROPE))
        outs["gdn_s"].append(st[2])
        outs["gc_s"].append(st[3])
        outs["fc_s"].append(st[4])

    y_prompt = xp.reshape(BP, LP, D)
    y_sample = xs.reshape(DB, TS, D)
    st = lambda k: jnp.stack(outs[k])
    return (y_prompt, y_sample, st("ckv_p"), st("kr_p"), st("ckv_s"), st("kr_s"),
            st("gdn_p"), st("gdn_s"), st("gc_p"), st("gc_s"), st("fc_p"), st("fc_s"))
```
